```python
import jax
import jax.numpy as jnp
from jax import lax
import numpy as np

D_MODEL = 2048
BATCH = 4
SEQ = 2048
DEPTH = 4
DEC_BATCH = 128
DEC_SEQ = 4
PAST_LEN = 16384
PAGE_SIZE = 128

N_MIXERS = 4
D_FF = 5632
EPS = 1e-6
CHUNK = 64
CONV_W = 4

RET_HEADS = 8
RET_DK = D_MODEL // RET_HEADS
RET_DV = 2 * D_MODEL // RET_HEADS
ROPE_BASE = 10000.0

ML_INNER = 2 * D_MODEL
ML_HEADS = 4
ML_DH = ML_INNER // ML_HEADS
ML_QKV_BLOCK = 4
ML_NBLK = ML_INNER // ML_QKV_BLOCK

RG_WIDTH = D_MODEL * 5 // 4
RG_HEADS = 10
RG_BLOCK = RG_WIDTH // RG_HEADS
RG_C = 8.0

SSD_INNER = 2 * D_MODEL
SSD_HEADDIM = 64
SSD_HEADS = SSD_INNER // SSD_HEADDIM
SSD_GROUPS = 8
SSD_REP = SSD_HEADS // SSD_GROUPS
SSD_STATE = 128
SSD_CONV_DIM = SSD_INNER + 2 * SSD_GROUPS * SSD_STATE

kernel_name = 'hybrid_ret_mlstm_rglru_ssd_macaron_step'

F32 = jnp.float32


def rmsnorm(x, w):
    xf = x.astype(F32)
    y = xf * lax.rsqrt(jnp.mean(xf * xf, axis=-1, keepdims=True) + EPS)
    return (y * w.astype(F32)).astype(x.dtype)


def head_layernorm(x, w):
    xf = x.astype(F32)
    mu = jnp.mean(xf, axis=-1, keepdims=True)
    xc = xf - mu
    y = xc * lax.rsqrt(jnp.mean(xc * xc, axis=-1, keepdims=True) + EPS)
    return y * w.astype(F32).reshape(x.shape[2], x.shape[3])


def swiglu(x, w_up, w_down):
    gate, up = jnp.split(x @ w_up, 2, axis=-1)
    return (jax.nn.silu(gate) * up) @ w_down


def causal_conv(x, buf, w, b):
    t = x.shape[1]
    xp = jnp.concatenate([buf.astype(x.dtype), x], axis=1)
    y = b
    for j in range(CONV_W):
        y = y + xp[:, j:j + t] * w[j]
    return y, xp[:, t:]


def block_diag(x, w):
    nb, bs, _ = w.shape
    xb = x.reshape(x.shape[:-1] + (nb, bs))
    return jnp.einsum('...ni,nio->...no', xb, w).reshape(x.shape)


def rotary(x, pos):
    half = x.shape[-1] // 2
    inv = ROPE_BASE ** (-jnp.arange(half, dtype=F32) / half)
    ang = pos.astype(F32)[:, None] * inv
    cos = jnp.cos(ang)[:, None, :]
    sin = jnp.sin(ang)[:, None, :]
    x1 = x[..., :half].astype(F32)
    x2 = x[..., half:].astype(F32)
    return jnp.concatenate([x1 * cos - x2 * sin, x1 * sin + x2 * cos], axis=-1).astype(x.dtype)


def chunk_len(t):
    return CHUNK if t % CHUNK == 0 else t


def to_chunks(a, l):
    b, t = a.shape[:2]
    return jnp.moveaxis(a.reshape((b, t // l, l) + a.shape[2:]), 1, 0)


def from_chunks(a):
    nc, b, l = a.shape[:3]
    return jnp.moveaxis(a, 0, 1).reshape((b, nc * l) + a.shape[3:])


def decay_scan(q, k, v, log_a, s0):
    t = q.shape[1]
    l = chunk_len(t)
    mask = jnp.tril(jnp.ones((l, l), dtype=bool))

    def step(s, inp):
        qc, kc, vc, la = inp
        b = jnp.cumsum(la, axis=1)
        bt = jnp.moveaxis(b, 1, -1)
        w = jnp.exp(jnp.where(mask, bt[..., :, None] - bt[..., None, :], -jnp.inf))
        qk = jnp.einsum('btgk,bsgk->bgts', qc, kc)
        y = jnp.einsum('bgrts,bsgrv->btgrv', qk[:, :, None] * w, vc)
        y = y + jnp.exp(b)[..., None] * jnp.einsum('btgk,bgrkv->btgrv', qc, s)
        b_last = b[:, -1]
        ws = jnp.exp(b_last[:, None] - b)
        s = jnp.exp(b_last)[..., None, None] * s + jnp.einsum('bsgk,bsgrv->bgrkv', kc, vc * ws[..., None])
        return s, y

    xs = (to_chunks(q.astype(F32), l), to_chunks(k.astype(F32), l),
          to_chunks(v.astype(F32), l), to_chunks(log_a.astype(F32), l))
    s, y = lax.scan(step, s0.astype(F32), xs)
    return from_chunks(y), s


def mlstm_scan(q, k, v, ig, lf, c0, n0, m0):
    t = q.shape[1]
    l = chunk_len(t)
    mask = jnp.tril(jnp.ones((l, l), dtype=bool))

    def step(carry, inp):
        c, n, m = carry
        qc, kc, vc, igc, lfc = inp
        b = jnp.moveaxis(jnp.cumsum(lfc, axis=1), 1, -1)
        ic = jnp.moveaxis(igc, 1, -1)
        d = jnp.where(mask, b[..., :, None] - b[..., None, :] + ic[..., None, :], -jnp.inf)
        inter = b + m[..., None]
        m_t = jnp.maximum(inter, jnp.max(d, axis=-1))
        sc = jnp.einsum('bthk,bshk->bhts', qc, kc) * jnp.exp(d - m_t[..., None])
        w_inter = jnp.moveaxis(jnp.exp(inter - m_t), -1, 1)
        num = jnp.einsum('bhts,bshv->bthv', sc, vc) + w_inter[..., None] * jnp.einsum('bthk,bhkv->bthv', qc, c)
        den = jnp.moveaxis(jnp.sum(sc, axis=-1), -1, 1) + w_inter * jnp.einsum('bthk,bhk->bth', qc, n)
        bound = jnp.exp(-jnp.moveaxis(m_t, -1, 1))
        h = num / jnp.maximum(jnp.abs(den), bound)[..., None]
        m_new = m_t[..., -1]
        ws = jnp.exp(b[..., -1:] - b + ic - m_new[..., None])
        decay = jnp.exp(b[..., -1] + m - m_new)
        kw = kc * jnp.moveaxis(ws, -1, 1)[..., None]
        c = decay[..., None, None] * c + jnp.einsum('bshk,bshv->bhkv', kw, vc)
        n = decay[..., None] * n + jnp.sum(kw, axis=1)
        return (c, n, m_new), h

    xs = (to_chunks(q.astype(F32), l), to_chunks(k.astype(F32), l), to_chunks(v.astype(F32), l),
          to_chunks(ig.astype(F32), l), to_chunks(lf.astype(F32), l))
    (c, n, m), h = lax.scan(step, (c0.astype(F32), n0.astype(F32), m0.astype(F32)), xs)
    return from_chunks(h), c, n, m


def retention_mixer(x, s0, pos, w_in, norm_w, w_out):
    bsz, t, _ = x.shape
    q, k, v, g = jnp.split(x @ w_in, [D_MODEL, 2 * D_MODEL, 4 * D_MODEL], axis=-1)
    q = rotary(q.reshape(bsz, t, RET_HEADS, RET_DK), pos)
    k = rotary(k.reshape(bsz, t, RET_HEADS, RET_DK), pos) * (RET_DK ** -0.5)
    v = v.reshape(bsz, t, RET_HEADS, 1, RET_DV)
    log_gamma = jnp.log1p(-(2.0 ** (-5.0 - jnp.arange(RET_HEADS, dtype=F32))))
    log_a = jnp.broadcast_to(log_gamma[:, None], (bsz, t, RET_HEADS, 1))
    y, s = decay_scan(q, k, v, log_a, s0[:, :, None])
    y = head_layernorm(y.reshape(bsz, t, RET_HEADS, RET_DV), norm_w).astype(x.dtype)
    out = (jax.nn.silu(g) * y.reshape(bsz, t, 2 * D_MODEL)) @ w_out
    return out, s[:, :, 0].astype(s0.dtype)


def mlstm_mixer(x, c0, n0, m0, conv0, w_up, conv_w, conv_b, w_q, w_k, w_v, w_gates, b_gates, skip, norm_w, w_down):
    bsz, t, _ = x.shape
    xm, z = jnp.split(x @ w_up, 2, axis=-1)
    xc, conv_new = causal_conv(xm, conv0, conv_w, conv_b)
    xc = jax.nn.silu(xc)
    q = block_diag(xc, w_q)
    k = block_diag(xc, w_k)
    v = block_diag(xm, w_v)
    gates = (jnp.concatenate([q, k, v], axis=-1) @ w_gates + b_gates).astype(F32)
    ig, fg = jnp.split(gates, 2, axis=-1)
    lf = jax.nn.log_sigmoid(fg)
    shp = (bsz, t, ML_HEADS, ML_DH)
    h, c, n, m = mlstm_scan(q.reshape(shp), k.reshape(shp) * (ML_DH ** -0.5), v.reshape(shp), ig, lf, c0, n0, m0)
    h = head_layernorm(h, norm_w).reshape(bsz, t, ML_INNER).astype(x.dtype)
    out = ((h + skip * xc) * jax.nn.silu(z)) @ w_down
    return out, c.astype(c0.dtype), n.astype(n0.dtype), m.astype(m0.dtype), conv_new


def _lin_combine(e, l):
    a1, b1 = e
    a2, b2 = l
    return a1 * a2, a2 * b1 + b2


def rglru_mixer(x, h0, conv0, w_in, conv_w, conv_b, w_a, b_a, w_x, b_x, lam, w_out):
    gate_branch, xr = jnp.split(x @ w_in, 2, axis=-1)
    gate_branch = jax.nn.gelu(gate_branch)
    xr, conv_new = causal_conv(xr, conv0, conv_w, conv_b)
    r = jax.nn.sigmoid(block_diag(xr, w_a) + b_a).astype(F32)
    i = jax.nn.sigmoid(block_diag(xr, w_x) + b_x).astype(F32)
    log_a = -RG_C * r * jax.nn.softplus(-lam.astype(F32))
    a = jnp.exp(log_a)
    bx = jnp.sqrt(-jnp.expm1(2.0 * log_a)) * (i * xr.astype(F32))
    a_cum, b_cum = lax.associative_scan(_lin_combine, (a, bx), axis=1)
    h = a_cum * h0.astype(F32)[:, None] + b_cum
    out = (h.astype(x.dtype) * gate_branch) @ w_out
    return out, h[:, -1].astype(h0.dtype), conv_new


def ssd_mixer(x, s0, conv0, w_in, conv_w, conv_b, dt_bias, a_log, d_skip, norm_w, w_out):
    bsz, t, _ = x.shape
    z, xbc, dt = jnp.split(x @ w_in, [SSD_INNER, SSD_INNER + SSD_CONV_DIM], axis=-1)
    xbc, conv_new = causal_conv(xbc, conv0, conv_w, conv_b)
    xbc = jax.nn.silu(xbc)
    xs, bm, cm = jnp.split(xbc, [SSD_INNER, SSD_INNER + SSD_GROUPS * SSD_STATE], axis=-1)
    xs = xs.reshape(bsz, t, SSD_GROUPS, SSD_REP, SSD_HEADDIM).astype(F32)
    bm = bm.reshape(bsz, t, SSD_GROUPS, SSD_STATE)
    cm = cm.reshape(bsz, t, SSD_GROUPS, SSD_STATE)
    dt = jax.nn.softplus(dt.astype(F32) + dt_bias.astype(F32)).reshape(bsz, t, SSD_GROUPS, SSD_REP)
    log_a = dt * (-jnp.exp(a_log.astype(F32))).reshape(SSD_GROUPS, SSD_REP)
    s_in = s0.reshape(bsz, SSD_GROUPS, SSD_REP, SSD_STATE, SSD_HEADDIM)
    y, s = decay_scan(cm, bm, xs * dt[..., None], log_a, s_in)
    y = y + d_skip.astype(F32).reshape(SSD_GROUPS, SSD_REP, 1) * xs
    y = y.reshape(bsz, t, SSD_INNER) * jax.nn.silu(z.astype(F32))
    yg = y.reshape(bsz, t, SSD_GROUPS, SSD_INNER // SSD_GROUPS)
    yg = yg * lax.rsqrt(jnp.mean(yg * yg, axis=-1, keepdims=True) + EPS)
    y = yg.reshape(bsz, t, SSD_INNER) * norm_w.astype(F32)
    out = y.astype(x.dtype) @ w_out
    return out, s.reshape(bsz, SSD_HEADS, SSD_STATE, SSD_HEADDIM).astype(s0.dtype), conv_new


def setup_inputs(seed: int = 0) -> dict:
    key = jax.random.key(seed)
    ks = iter(jax.random.split(key, 80))

    def nrm(shape, scale):
        return jax.random.normal(next(ks), shape, F32) * scale

    def gain(shape):
        return 1.0 + nrm(shape, 0.02)

    def unif(shape, lo, hi):
        return jax.random.uniform(next(ks), shape, F32, lo, hi)

    d = D_MODEL
    u_lam = unif((RG_WIDTH,), 0.9, 0.999)
    dt0 = jnp.exp(unif((SSD_HEADS,), float(np.log(1e-3)), float(np.log(1e-1))))
    ssd_in_cols = SSD_INNER + SSD_CONV_DIM + SSD_HEADS
    return {
        'x_prompt': nrm((BATCH, SEQ, d), 1.0),
        'x_sample': nrm((DEC_BATCH, DEC_SEQ, d), 1.0),
        'state_ret': nrm((DEC_BATCH, RET_HEADS, RET_DK, RET_DV), 0.5),
        'state_mlstm_c': nrm((DEC_BATCH, ML_HEADS, ML_DH, ML_DH), 0.05),
        'state_mlstm_n': nrm((DEC_BATCH, ML_HEADS, ML_DH), 0.5),
        'state_mlstm_m': 2.0 + nrm((DEC_BATCH, ML_HEADS), 0.5),
        'cache_mlstm_conv': nrm((DEC_BATCH, CONV_W - 1, ML_INNER), 1.0),
        'state_rglru_h': nrm((DEC_BATCH, RG_WIDTH), 0.5),
        'cache_rglru_conv': nrm((DEC_BATCH, CONV_W - 1, RG_WIDTH), 1.0),
        'state_ssd': nrm((DEC_BATCH, SSD_HEADS, SSD_STATE, SSD_HEADDIM), 0.1),
        'cache_ssd_conv': nrm((DEC_BATCH, CONV_W - 1, SSD_CONV_DIM), 1.0),
        'norm_ffn1': gain((DEPTH, d)),
        'w_ffn1_up': nrm((DEPTH, d, 2 * D_FF), d ** -0.5),
        'w_ffn1_down': nrm((DEPTH, D_FF, d), D_FF ** -0.5),
        'norm_mix': gain((DEPTH, d)),
        'norm_ffn2': gain((DEPTH, d)),
        'w_ffn2_up': nrm((DEPTH, d, 2 * D_FF), d ** -0.5),
        'w_ffn2_down': nrm((DEPTH, D_FF, d), D_FF ** -0.5),
        'norm_final': gain((d,)),
        'ret_w_in': nrm((d, 6 * d), d ** -0.5),
        'ret_norm': gain((2 * d,)),
        'ret_w_out': nrm((2 * d, d), (2 * d) ** -0.5),
        'ml_w_up': nrm((d, 2 * ML_INNER), d ** -0.5),
        'ml_conv_w': nrm((CONV_W, ML_INNER), 0.5),
        'ml_conv_b': nrm((ML_INNER,), 0.02),
        'ml_w_q': nrm((ML_NBLK, ML_QKV_BLOCK, ML_QKV_BLOCK), 0.5),
        'ml_w_k': nrm((ML_NBLK, ML_QKV_BLOCK, ML_QKV_BLOCK), 0.5),
        'ml_w_v': nrm((ML_NBLK, ML_QKV_BLOCK, ML_QKV_BLOCK), 0.5),
        'ml_w_gates': nrm((3 * ML_INNER, 2 * ML_HEADS), 0.5 * (3 * ML_INNER) ** -0.5),
        'ml_b_gates': jnp.concatenate([nrm((ML_HEADS,), 0.1), jnp.linspace(3.0, 6.0, ML_HEADS) + nrm((ML_HEADS,), 0.1)]),
        'ml_skip': gain((ML_INNER,)),
        'ml_norm': gain((ML_INNER,)),
        'ml_w_down': nrm((ML_INNER, d), ML_INNER ** -0.5),
        'rg_w_in': nrm((d, 2 * RG_WIDTH), d ** -0.5),
        'rg_conv_w': nrm((CONV_W, RG_WIDTH), 0.5),
        'rg_conv_b': nrm((RG_WIDTH,), 0.02),
        'rg_w_a': nrm((RG_HEADS, RG_BLOCK, RG_BLOCK), RG_BLOCK ** -0.5),
        'rg_b_a': nrm((RG_WIDTH,), 0.1),
        'rg_w_x': nrm((RG_HEADS, RG_BLOCK, RG_BLOCK), RG_BLOCK ** -0.5),
        'rg_b_x': nrm((RG_WIDTH,), 0.1),
        'rg_lambda': jnp.log(u_lam) - jnp.log1p(-u_lam),
        'rg_w_out': nrm((RG_WIDTH, d), RG_WIDTH ** -0.5),
        'ssd_w_in': nrm((d, ssd_in_cols), d ** -0.5),
        'ssd_conv_w': nrm((CONV_W, SSD_CONV_DIM), 0.5),
        'ssd_conv_b': nrm((SSD_CONV_DIM,), 0.02),
        'ssd_dt_bias': dt0 + jnp.log(-jnp.expm1(-dt0)),
        'ssd_a_log': jnp.log(unif((SSD_HEADS,), 1.0, 16.0)),
        'ssd_d': gain((SSD_HEADS,)),
        'ssd_norm': gain((SSD_INNER,)),
        'ssd_w_out': nrm((SSD_INNER, d), SSD_INNER ** -0.5),
    }


def reference(x_prompt, x_sample, state_ret, state_mlstm_c, state_mlstm_n, state_mlstm_m, cache_mlstm_conv,
              state_rglru_h, cache_rglru_conv, state_ssd, cache_ssd_conv,
              norm_ffn1, w_ffn1_up, w_ffn1_down, norm_mix, norm_ffn2, w_ffn2_up, w_ffn2_down, norm_final,
              ret_w_in, ret_norm, ret_w_out,
              ml_w_up, ml_conv_w, ml_conv_b, ml_w_q, ml_w_k, ml_w_v, ml_w_gates, ml_b_gates, ml_skip, ml_norm, ml_w_down,
              rg_w_in, rg_conv_w, rg_conv_b, rg_w_a, rg_b_a, rg_w_x, rg_b_x, rg_lambda, rg_w_out,
              ssd_w_in, ssd_conv_w, ssd_conv_b, ssd_dt_bias, ssd_a_log, ssd_d, ssd_norm, ssd_w_out):

    def run(x, pos, st):
        ret_s, ml_c, ml_n, ml_m, ml_cv, rg_h, rg_cv, ssd_s, ssd_cv = st
        for i in range(DEPTH):
            x = x + 0.5 * swiglu(rmsnorm(x, norm_ffn1[i]), w_ffn1_up[i], w_ffn1_down[i])
            h = rmsnorm(x, norm_mix[i])
            kind = i % N_MIXERS
            if kind == 0:
                y, ret_s = retention_mixer(h, ret_s, pos, ret_w_in, ret_norm, ret_w_out)
            elif kind == 1:
                y, ml_c, ml_n, ml_m, ml_cv = mlstm_mixer(h, ml_c, ml_n, ml_m, ml_cv, ml_w_up, ml_conv_w, ml_conv_b,
                                                         ml_w_q, ml_w_k, ml_w_v, ml_w_gates, ml_b_gates, ml_skip,
                                                         ml_norm, ml_w_down)
            elif kind == 2:
                y, rg_h, rg_cv = rglru_mixer(h, rg_h, rg_cv, rg_w_in, rg_conv_w, rg_conv_b, rg_w_a, rg_b_a,
                                             rg_w_x, rg_b_x, rg_lambda, rg_w_out)
            else:
                y, ssd_s, ssd_cv = ssd_mixer(h, ssd_s, ssd_cv, ssd_w_in, ssd_conv_w, ssd_conv_b, ssd_dt_bias,
                                             ssd_a_log, ssd_d, ssd_norm, ssd_w_out)
            x = x + y
            x = x + 0.5 * swiglu(rmsnorm(x, norm_ffn2[i]), w_ffn2_up[i], w_ffn2_down[i])
        return rmsnorm(x, norm_final), (ret_s, ml_c, ml_n, ml_m, ml_cv, rg_h, rg_cv, ssd_s, ssd_cv)

    bp = x_prompt.shape[0]
    dtp = x_prompt.dtype

    def zeros(*shape):
        return jnp.zeros((bp,) + shape, dtp)

    st_prompt = (zeros(RET_HEADS, RET_DK, RET_DV), zeros(ML_HEADS, ML_DH, ML_DH), zeros(ML_HEADS, ML_DH),
                 zeros(ML_HEADS), zeros(CONV_W - 1, ML_INNER), zeros(RG_WIDTH), zeros(CONV_W - 1, RG_WIDTH),
                 zeros(SSD_HEADS, SSD_STATE, SSD_HEADDIM), zeros(CONV_W - 1, SSD_CONV_DIM))
    y_prompt, (p_ret, p_mlc, p_mln, p_mlm, p_mlcv, p_rgh, p_rgcv, p_ssd, p_ssdcv) = run(
        x_prompt, jnp.arange(x_prompt.shape[1]), st_prompt)
    st_sample = (state_ret, state_mlstm_c, state_mlstm_n, state_mlstm_m, cache_mlstm_conv,
                 state_rglru_h, cache_rglru_conv, state_ssd, cache_ssd_conv)
    y_sample, (s_ret, s_mlc, s_mln, s_mlm, s_mlcv, s_rgh, s_rgcv, s_ssd, s_ssdcv) = run(
        x_sample, PAST_LEN + jnp.arange(x_sample.shape[1]), st_sample)
    return (y_prompt, y_sample,
            p_ret, p_mlc, p_mln, p_mlm, p_mlcv, p_rgh, p_rgcv, p_ssd, p_ssdcv,
            s_ret, s_mlc, s_mln, s_mlm, s_mlcv, s_rgh, s_rgcv, s_ssd, s_ssdcv)
```

```python
import functools
import math

import jax
import jax.numpy as jnp
from jax import lax
from jax.experimental import pallas as pl
from jax.experimental.pallas import tpu as pltpu

D_MODEL = 2048
BATCH = 4
SEQ = 2048
DEPTH = 4
DEC_BATCH = 128
DEC_SEQ = 4
PAST_LEN = 16384
N_MIXERS = 4
D_FF = 5632
EPS = 1e-6
CONV_W = 4

RET_HEADS = 8
ROPE_BASE = 10000.0
ML_HEADS = 4
ML_QKV_BLOCK = 4
RG_HEADS = 10
RG_C = 8.0
SSD_HEADDIM = 64
SSD_GROUPS = 8
SSD_STATE = 128

F32 = jnp.float32
MXU_DTYPE = jnp.bfloat16
HIGHEST = lax.Precision.HIGHEST

SUBLANES = 8
LANES = 128
MXU_DIM = 256
VMEM_LIMIT = 56 * 1024 * 1024

SAMPLE_ROWS = SUBLANES
SAMPLE_PAD = SAMPLE_ROWS - DEC_SEQ
HALO = SUBLANES

ROW_TILE = 512
COL_TILE = 512
PROMPT_CHUNK = 256
PRE_ROW_TILE = 128


def _dims():
    d = D_MODEL
    ret_dk = d // RET_HEADS
    ml_inner = 2 * d
    rg_width = d * 5 // 4
    ssd_inner = 2 * d
    ssd_heads = ssd_inner // SSD_HEADDIM
    return dict(
        ret_dk=ret_dk, ret_dv=2 * ret_dk,
        ml_inner=ml_inner, ml_dh=ml_inner // ML_HEADS,
        rg_width=rg_width, rg_block=rg_width // RG_HEADS,
        ssd_inner=ssd_inner, ssd_heads=ssd_heads, ssd_rep=ssd_heads // SSD_GROUPS,
        ssd_conv=ssd_inner + 2 * SSD_GROUPS * SSD_STATE,
        ntp=BATCH * SEQ, nts=DEC_BATCH * SAMPLE_ROWS,
    )


def _tile(n, pref, mult):
    t = min(pref, n) // mult * mult
    while t > mult and n % t:
        t -= mult
    assert t >= mult and n % t == 0, (n, pref, mult)
    return t


def _cparams(sem):
    return pltpu.CompilerParams(dimension_semantics=sem, vmem_limit_bytes=VMEM_LIMIT)


def _silu(x):
    return x * jax.nn.sigmoid(x)


def _softplus(x):
    return jnp.maximum(x, 0.0) + jnp.log1p(jnp.exp(-jnp.abs(x)))


def _dot(a, b):
    return jnp.dot(a.astype(MXU_DTYPE), b.astype(MXU_DTYPE), preferred_element_type=F32)


def _dot_nt(a, b):
    return lax.dot_general(a.astype(MXU_DTYPE), b.astype(MXU_DTYPE), (((1,), (1,)), ((), ())),
                           preferred_element_type=F32)


def _dot_tn(a, b):
    return lax.dot_general(a.astype(MXU_DTYPE), b.astype(MXU_DTYPE), (((0,), (0,)), ((), ())),
                           preferred_element_type=F32)


def _dot_exact(a, b):
    return jnp.dot(a, b, preferred_element_type=F32, precision=HIGHEST)


def _dot_nt_exact(a, b):
    return lax.dot_general(a, b, (((1,), (1,)), ((), ())), preferred_element_type=F32, precision=HIGHEST)


def _cumsum_rows(x):
    n = x.shape[0]
    row = lax.broadcasted_iota(jnp.int32, x.shape, 0)
    sh = 1
    while sh < n:
        x = x + jnp.where(row >= sh, pltpu.roll(x, sh, 0), 0.0)
        sh *= 2
    return x


def _linear_scan_rows(a, b, span, pos):
    sh = 1
    while sh < span:
        ok = pos >= sh
        a_prev = jnp.where(ok, pltpu.roll(a, sh, 0), 1.0)
        b_prev = jnp.where(ok, pltpu.roll(b, sh, 0), 0.0)
        b = b + a * b_prev
        a = a * a_prev
        sh *= 2
    return a, b


def _causal_conv(x, halo, win_ref, w_ref, b_ref):
    tm = x.shape[0]
    win_ref[0:HALO, :] = halo
    win_ref[HALO:HALO + tm, :] = x
    y = b_ref[...] + x * w_ref[CONV_W - 1:CONV_W, :]
    for j in range(1, CONV_W):
        y = y + win_ref[pl.ds(HALO - j, tm), :] * w_ref[CONV_W - 1 - j:CONV_W - j, :]
    return y


def _token_rows(n):
    row = lax.broadcasted_iota(jnp.int32, (n, 1), 0)
    return (row % SAMPLE_ROWS) >= SAMPLE_PAD


def _rmsnorm_kernel(x_ref, w_ref, o_ref):
    x = x_ref[...]
    ms = jnp.mean(x * x, axis=-1, keepdims=True)
    o_ref[...] = (x * lax.rsqrt(ms + EPS) * w_ref[...]).astype(o_ref.dtype)


def _rmsnorm(x, w, layer, out_dtype):
    m, d = x.shape
    tm = _tile(m, ROW_TILE, 16)
    return pl.pallas_call(
        _rmsnorm_kernel,
        grid=(m // tm,),
        in_specs=[pl.BlockSpec((tm, d), lambda i: (i, 0)),
                  pl.BlockSpec((None, 1, d), lambda i: (layer, 0, 0))],
        out_specs=pl.BlockSpec((tm, d), lambda i: (i, 0)),
        out_shape=jax.ShapeDtypeStruct((m, d), out_dtype),
        compiler_params=_cparams(("parallel",)),
        name="rmsnorm",
    )(x, w)


def _mm_kernel(a_ref, w_ref, *rest, scale, has_res):
    if has_res:
        r_ref, o_ref, wb_ref = rest
    else:
        o_ref, wb_ref = rest

    @pl.when(pl.program_id(1) == 0)
    def _():
        wb_ref[...] = w_ref[...].astype(MXU_DTYPE)

    acc = jnp.dot(a_ref[...].astype(MXU_DTYPE), wb_ref[...], preferred_element_type=F32)
    if has_res:
        acc = r_ref[...] + scale * acc
    o_ref[...] = acc.astype(o_ref.dtype)


def _mm(a, w, layer, *, res=None, scale=1.0, tm_pref=ROW_TILE, tn_pref=COL_TILE):
    m, k = a.shape
    n = w.shape[2]
    tm = _tile(m, tm_pref, 16)
    tn = min(tn_pref, n)
    nj = pl.cdiv(n, tn)
    in_specs = [pl.BlockSpec((tm, k), lambda j, i: (i, 0)),
                pl.BlockSpec((None, k, tn), lambda j, i: (layer, 0, j))]
    args = [a, w]
    aliases = {}
    if res is not None:
        assert n % tn == 0 and res.shape == (m, n)
        in_specs.append(pl.BlockSpec((tm, tn), lambda j, i: (i, j)))
        args.append(res)
        aliases = {2: 0}
    return pl.pallas_call(
        functools.partial(_mm_kernel, scale=scale, has_res=res is not None),
        grid=(nj, m // tm),
        in_specs=in_specs,
        out_specs=pl.BlockSpec((tm, tn), lambda j, i: (i, j)),
        out_shape=jax.ShapeDtypeStruct((m, nj * tn), F32),
        scratch_shapes=[pltpu.VMEM((k, tn), MXU_DTYPE)],
        input_output_aliases=aliases,
        compiler_params=_cparams(("parallel", "arbitrary")),
        name="matmul",
    )(*args)


def _swiglu_up_kernel(a_ref, wg_ref, wu_ref, o_ref, wgb_ref, wub_ref):
    @pl.when(pl.program_id(1) == 0)
    def _():
        wgb_ref[...] = wg_ref[...].astype(MXU_DTYPE)
        wub_ref[...] = wu_ref[...].astype(MXU_DTYPE)

    a = a_ref[...]
    gate = jnp.dot(a, wgb_ref[...], preferred_element_type=F32)
    up = jnp.dot(a, wub_ref[...], preferred_element_type=F32)
    o_ref[...] = (_silu(gate) * up).astype(o_ref.dtype)


def _swiglu_up(a, w_up, layer):
    m, k = a.shape
    ff = w_up.shape[2] // 2
    tm = _tile(m, ROW_TILE, 16)
    tn = _tile(ff, COL_TILE, LANES)
    nj = ff // tn
    return pl.pallas_call(
        _swiglu_up_kernel,
        grid=(nj, m // tm),
        in_specs=[pl.BlockSpec((tm, k), lambda j, i: (i, 0)),
                  pl.BlockSpec((None, k, tn), lambda j, i: (layer, 0, j)),
                  pl.BlockSpec((None, k, tn), lambda j, i: (layer, 0, nj + j))],
        out_specs=pl.BlockSpec((tm, tn), lambda j, i: (i, j)),
        out_shape=jax.ShapeDtypeStruct((m, ff), MXU_DTYPE),
        scratch_shapes=[pltpu.VMEM((k, tn), MXU_DTYPE), pltpu.VMEM((k, tn), MXU_DTYPE)],
        compiler_params=_cparams(("parallel", "arbitrary")),
        name="swiglu_up",
    )(a, w_up, w_up)


def _ffn(x, norm_w, w_up, w_down, layer):
    h = _rmsnorm(x, norm_w, layer, MXU_DTYPE)
    act = _swiglu_up(h, w_up, layer)
    return _mm(act, w_down, layer, res=x, scale=0.5, tm_pref=256)


def _ret_scan_kernel(lg_ref, q_ref, k_ref, v_ref, g_ref, cos_ref, sin_ref, nw_ref, *rest,
                     chunk, pad, has_init, has_ybuf, k_scale):
    rest = list(rest)
    s0_ref = rest.pop(0) if has_init else None
    if has_ybuf:
        rest.pop(0)
    y_ref, s_out_ref, s_ref = rest
    h = pl.program_id(1)
    c = pl.program_id(2)
    nvalid = chunk - pad

    @pl.when(c == 0)
    def _():
        s_ref[...] = s0_ref[0, 0] if has_init else jnp.zeros(s_ref.shape, F32)

    lam = lg_ref[h]
    cos = cos_ref[...]
    sin = sin_ref[...]
    half = cos.shape[1]

    def rot(x):
        x1 = x[:, :half]
        x2 = x[:, half:]
        return jnp.concatenate([x1 * cos - x2 * sin, x1 * sin + x2 * cos], axis=-1)

    q = rot(q_ref[...])
    k = rot(k_ref[...]) * k_scale
    v = v_ref[...]

    t_col = lax.broadcasted_iota(jnp.int32, (chunk, 1), 0)
    cnt_col = jnp.maximum(t_col - pad + 1, 0).astype(F32)
    t_mat = lax.broadcasted_iota(jnp.int32, (chunk, chunk), 0)
    s_mat = lax.broadcasted_iota(jnp.int32, (chunk, chunk), 1)
    cnt_t = jnp.maximum(t_mat - pad + 1, 0).astype(F32)
    cnt_s = jnp.maximum(s_mat - pad + 1, 0).astype(F32)
    w = jnp.where((t_mat >= s_mat) & (s_mat >= pad), jnp.exp(lam * (cnt_t - cnt_s)), 0.0)

    s = s_ref[...]
    y = _dot(_dot_nt(q, k) * w, v) + jnp.exp(lam * cnt_col) * _dot(q, s)
    ws = jnp.where(t_col >= pad, jnp.exp(lam * (nvalid - cnt_col)), 0.0)
    s_new = jnp.exp(lam * nvalid) * s + _dot_tn(k * ws, v)
    s_ref[...] = s_new

    mu = jnp.mean(y, axis=-1, keepdims=True)
    yc = y - mu
    yn = yc * lax.rsqrt(jnp.mean(yc * yc, axis=-1, keepdims=True) + EPS) * nw_ref[...]
    out = _silu(g_ref[...]) * yn
    y_ref[...] = jnp.where(t_col >= pad, out, 0.0).astype(y_ref.dtype)

    @pl.when(c == pl.num_programs(2) - 1)
    def _():
        s_out_ref[0, 0] = s_new


def _ret_scan(proj, cos, sin, log_gamma, norm_w, s0, ybuf, *, nb, nc, chunk, pad, row_blk0):
    dm = _dims()
    dk, dv, nh = dm["ret_dk"], dm["ret_dv"], RET_HEADS
    nt = proj.shape[0]

    def rows(b, c):
        return row_blk0 + b * nc + c

    in_specs = [pl.BlockSpec(memory_space=pltpu.SMEM),
                pl.BlockSpec((chunk, dk), lambda b, h, c: (rows(b, c), h)),
                pl.BlockSpec((chunk, dk), lambda b, h, c: (rows(b, c), nh + h)),
                pl.BlockSpec((chunk, dv), lambda b, h, c: (rows(b, c), nh + h)),
                pl.BlockSpec((chunk, dv), lambda b, h, c: (rows(b, c), 2 * nh + h)),
                pl.BlockSpec((chunk, dk // 2), lambda b, h, c: (rows(b, c), 0)),
                pl.BlockSpec((chunk, dk // 2), lambda b, h, c: (rows(b, c), 0)),
                pl.BlockSpec((1, dv), lambda b, h, c: (0, h))]
    args = [log_gamma, proj, proj, proj, proj, cos, sin, norm_w]
    aliases = {}
    if s0 is not None:
        in_specs.append(pl.BlockSpec((1, 1, dk, dv), lambda b, h, c: (b, h, 0, 0)))
        args.append(s0)
    if ybuf is not None:
        in_specs.append(pl.BlockSpec(memory_space=pl.ANY))
        args.append(ybuf)
        aliases = {len(args) - 1: 0}
    return pl.pallas_call(
        functools.partial(_ret_scan_kernel, chunk=chunk, pad=pad, has_init=s0 is not None,
                          has_ybuf=ybuf is not None, k_scale=dk ** -0.5),
        grid=(nb, nh, nc),
        in_specs=in_specs,
        out_specs=[pl.BlockSpec((chunk, dv), lambda b, h, c: (rows(b, c), h)),
                   pl.BlockSpec((1, 1, dk, dv), lambda b, h, c: (b, h, 0, 0))],
        out_shape=[jax.ShapeDtypeStruct((nt, nh * dv), F32),
                   jax.ShapeDtypeStruct((nb, nh, dk, dv), F32)],
        scratch_shapes=[pltpu.VMEM((dk, dv), F32)],
        input_output_aliases=aliases,
        compiler_params=_cparams(("parallel", "parallel", "arbitrary")),
        name="retention_scan",
    )(*args)


def _retention_mixer(hn, x, state_ret, w_in, norm_w, w_out):
    dm = _dims()
    ntp, nts = dm["ntp"], dm["nts"]
    dk = dm["ret_dk"]
    proj = _mm(hn, w_in[None], 0)

    half = dk // 2
    inv = ROPE_BASE ** (-jnp.arange(half, dtype=F32) / half)
    pos_p = jnp.tile(jnp.arange(SEQ), BATCH)
    pos_s = jnp.tile(PAST_LEN - SAMPLE_PAD + jnp.arange(SAMPLE_ROWS), DEC_BATCH)
    ang = jnp.concatenate([pos_p, pos_s]).astype(F32)[:, None] * inv
    cos, sin = jnp.cos(ang), jnp.sin(ang)
    log_gamma = jnp.log1p(-(2.0 ** (-5.0 - jnp.arange(RET_HEADS, dtype=F32))))
    nw = norm_w.reshape(1, -1)

    chunk = _tile(SEQ, PROMPT_CHUNK, SUBLANES)
    y, s_p = _ret_scan(proj, cos, sin, log_gamma, nw, None, None,
                       nb=BATCH, nc=SEQ // chunk, chunk=chunk, pad=0, row_blk0=0)
    y, s_s = _ret_scan(proj, cos, sin, log_gamma, nw, state_ret, y,
                       nb=DEC_BATCH, nc=1, chunk=SAMPLE_ROWS, pad=SAMPLE_PAD, row_blk0=ntp // SAMPLE_ROWS)
    x = _mm(y, w_out[None], 0, res=x, scale=1.0, tm_pref=256)
    return x, s_p, s_s


def _ml_pre_kernel(xm_ref, halo_ref, *rest, sample, seq_len, k_scale, nblk, bw):
    rest = list(rest)
    cache_ref = rest.pop(0) if sample else None
    (cw_ref, cb_ref, wq_ref, wk_ref, wv_ref, wg_ref, bg_ref,
     q_ref, k_ref, v_ref, xc_ref, gates_ref, win_ref) = rest
    tm = xm_ref.shape[0]
    xm = xm_ref[...]
    if sample:
        xm = jnp.where(_token_rows(tm), xm, cache_ref[...])
        halo = jnp.zeros(halo_ref.shape, F32)
    else:
        seq_start = (pl.program_id(0) * tm) % seq_len == 0
        halo = jnp.where(seq_start, 0.0, halo_ref[...])
    xc = _silu(_causal_conv(xm, halo, win_ref, cw_ref, cb_ref))
    xc_ref[...] = xc

    for j in range(nblk):
        sl = slice(j * bw, (j + 1) * bw)
        q_ref[:, sl] = _dot(xc[:, sl], wq_ref[j])
        k_ref[:, sl] = _dot(xc[:, sl], wk_ref[j])
        v_ref[:, sl] = _dot(xm[:, sl], wv_ref[j])
    inner = q_ref.shape[1]
    g = (_dot(q_ref[...], wg_ref[0:inner, :]) + _dot(k_ref[...], wg_ref[inner:2 * inner, :])
         + _dot(v_ref[...], wg_ref[2 * inner:3 * inner, :]) + bg_ref[...])
    k_ref[...] = k_ref[...] * k_scale
    lane = lax.broadcasted_iota(jnp.int32, g.shape, 1)
    log_f = jnp.minimum(g, 0.0) - jnp.log1p(jnp.exp(-jnp.abs(g)))
    gates_ref[...] = jnp.where(lane < ML_HEADS, g, jnp.where(lane < 2 * ML_HEADS, log_f, 0.0))


def _ml_pre(up, cache_rows, cw, cb, wq, wk, wv, wg, bg, *, sample):
    dm = _dims()
    inner, dh = dm["ml_inner"], dm["ml_dh"]
    n = dm["nts"] if sample else dm["ntp"]
    tm = _tile(n if sample else SEQ, PRE_ROW_TILE, SAMPLE_ROWS)
    blk0 = dm["ntp"] // tm if sample else 0
    hb = tm // HALO
    nblk, bw = wq.shape[0], wq.shape[1]

    in_specs = [pl.BlockSpec((tm, inner), lambda i: (blk0 + i, 0)),
                pl.BlockSpec((HALO, inner), lambda i: (jnp.maximum((blk0 + i) * hb - 1, 0), 0))]
    args = [up, up]
    if sample:
        in_specs.append(pl.BlockSpec((tm, inner), lambda i: (i, 0)))
        args.append(cache_rows)
    const2 = lambda i: (0, 0)
    const3 = lambda i: (0, 0, 0)
    in_specs += [pl.BlockSpec(cw.shape, const2), pl.BlockSpec(cb.shape, const2),
                 pl.BlockSpec(wq.shape, const3), pl.BlockSpec(wk.shape, const3), pl.BlockSpec(wv.shape, const3),
                 pl.BlockSpec(wg.shape, const2), pl.BlockSpec(bg.shape, const2)]
    args += [cw, cb, wq, wk, wv, wg, bg]
    row_spec = pl.BlockSpec((tm, inner), lambda i: (i, 0))
    return pl.pallas_call(
        functools.partial(_ml_pre_kernel, sample=sample, seq_len=SEQ, k_scale=dh ** -0.5, nblk=nblk, bw=bw),
        grid=(n // tm,),
        in_specs=in_specs,
        out_specs=[row_spec, row_spec, row_spec, row_spec, pl.BlockSpec((tm, LANES), lambda i: (i, 0))],
        out_shape=[jax.ShapeDtypeStruct((n, inner), F32)] * 4 + [jax.ShapeDtypeStruct((n, LANES), F32)],
        scratch_shapes=[pltpu.VMEM((HALO + tm, inner), F32)],
        compiler_params=_cparams(("arbitrary",)),
        name="mlstm_pre",
    )(*args)


def _ml_scan_kernel(m0_ref, q_ref, k_ref, v_ref, xc_ref, z_ref, gates_ref, skip_ref, nw_ref, *rest,
                    chunk, pad, has_init, has_ybuf):
    rest = list(rest)
    if has_init:
        c0_ref, n0_ref = rest.pop(0), rest.pop(0)
    if has_ybuf:
        rest.pop(0)
    y_ref, c_out_ref, n_out_ref, m_out_ref, c_ref, n_ref, m_ref = rest
    b = pl.program_id(0)
    h = pl.program_id(1)
    c = pl.program_id(2)

    @pl.when(c == 0)
    def _():
        if has_init:
            c_ref[...] = c0_ref[0, 0]
            n_ref[...] = n0_ref[0, 0]
            m_ref[...] = jnp.full(m_ref.shape, m0_ref[b * ML_HEADS + h], F32)
        else:
            c_ref[...] = jnp.zeros(c_ref.shape, F32)
            n_ref[...] = jnp.zeros(n_ref.shape, F32)
            m_ref[...] = jnp.zeros(m_ref.shape, F32)

    q = q_ref[...]
    k = k_ref[...]
    v = v_ref[...]
    t_col = lax.broadcasted_iota(jnp.int32, (chunk, 1), 0)
    valid_col = t_col >= pad

    gates = jnp.where(valid_col, gates_ref[...], 0.0)
    csum = _cumsum_rows(gates)
    lane = lax.broadcasted_iota(jnp.int32, gates.shape, 1)
    ig_col = jnp.sum(jnp.where(lane == h, gates, 0.0), axis=1, keepdims=True)
    b_col = jnp.sum(jnp.where(lane == ML_HEADS + h, csum, 0.0), axis=1, keepdims=True)
    ig_col = jnp.where(valid_col, ig_col, -jnp.inf)
    sel = (lax.broadcasted_iota(jnp.int32, (SUBLANES, LANES), 0)
           == lax.broadcasted_iota(jnp.int32, (SUBLANES, LANES), 1)).astype(F32)
    sub = lax.broadcasted_iota(jnp.int32, (SUBLANES, chunk), 0)
    ig_row = jnp.sum(jnp.where(sub == h, _dot_nt_exact(sel, gates), 0.0), axis=0, keepdims=True)
    b_row = jnp.sum(jnp.where(sub == ML_HEADS + h, _dot_nt_exact(sel, csum), 0.0), axis=0, keepdims=True)
    s_row = lax.broadcasted_iota(jnp.int32, (1, chunk), 1)
    ig_row = jnp.where(s_row >= pad, ig_row, -jnp.inf)

    t_mat = lax.broadcasted_iota(jnp.int32, (chunk, chunk), 0)
    s_mat = lax.broadcasted_iota(jnp.int32, (chunk, chunk), 1)
    d = jnp.where(t_mat >= s_mat, b_col - b_row + ig_row, -jnp.inf)
    m_prev = m_ref[...][:, 0:1]
    inter = b_col + m_prev
    m_t = jnp.maximum(inter, jnp.max(d, axis=1, keepdims=True))
    sc = _dot_nt(q, k) * jnp.exp(d - m_t)
    w_inter = jnp.exp(inter - m_t)
    cmat = c_ref[...]
    nvec = n_ref[...]
    num = _dot(sc, v) + w_inter * _dot(q, cmat)
    den = jnp.sum(sc, axis=1, keepdims=True) + w_inter * jnp.sum(q * nvec, axis=1, keepdims=True)
    hid = num / jnp.maximum(jnp.abs(den), jnp.exp(-m_t))

    m_new = m_t[chunk - 1:chunk, :]
    b_last = b_col[chunk - 1:chunk, :]
    ws = jnp.exp(b_last - b_col + ig_col - m_new)
    decay = jnp.exp(b_last + m_prev - m_new)
    kw = k * ws
    c_new = decay * cmat + _dot_tn(kw, v)
    n_new = decay * nvec + jnp.sum(kw, axis=0, keepdims=True)
    c_ref[...] = c_new
    n_ref[...] = n_new
    m_ref[...] = jnp.broadcast_to(m_new, m_ref.shape)

    mu = jnp.mean(hid, axis=-1, keepdims=True)
    hc = hid - mu
    hn = hc * lax.rsqrt(jnp.mean(hc * hc, axis=-1, keepdims=True) + EPS) * nw_ref[...]
    out = (hn + skip_ref[...] * xc_ref[...]) * _silu(z_ref[...])
    y_ref[...] = jnp.where(valid_col, out, 0.0).astype(y_ref.dtype)

    @pl.when(c == pl.num_programs(2) - 1)
    def _():
        c_out_ref[0, 0] = c_new
        n_out_ref[0, 0] = n_new
        m_out_ref[0, 0] = jnp.broadcast_to(m_new, (1, LANES))


def _ml_scan(q, k, v, xc, up, gates, skip, norm_w, init, ybuf, *, nb, nc, chunk, pad, row_blk0):
    dm = _dims()
    dh, nh, inner = dm["ml_dh"], ML_HEADS, dm["ml_inner"]
    nt = up.shape[0]

    def loc(b, h, c):
        return (b * nc + c, h)

    head_spec = pl.BlockSpec((chunk, dh), loc)
    in_specs = [pl.BlockSpec(memory_space=pltpu.SMEM),
                head_spec, head_spec, head_spec, head_spec,
                pl.BlockSpec((chunk, dh), lambda b, h, c: (row_blk0 + b * nc + c, nh + h)),
                pl.BlockSpec((chunk, LANES), lambda b, h, c: (b * nc + c, 0)),
                pl.BlockSpec((1, dh), lambda b, h, c: (0, h)),
                pl.BlockSpec((1, dh), lambda b, h, c: (0, h))]
    has_init = init is not None
    if has_init:
        c0, n0, m0 = init
    else:
        c0 = n0 = None
        m0 = jnp.zeros((1,), F32)
    args = [m0, q, k, v, xc, up, gates, skip, norm_w]
    if has_init:
        in_specs += [pl.BlockSpec((1, 1, dh, dh), lambda b, h, c: (b, h, 0, 0)),
                     pl.BlockSpec((1, 1, 1, dh), lambda b, h, c: (b, h, 0, 0))]
        args += [c0, n0]
    aliases = {}
    if ybuf is not None:
        in_specs.append(pl.BlockSpec(memory_space=pl.ANY))
        args.append(ybuf)
        aliases = {len(args) - 1: 0}
    return pl.pallas_call(
        functools.partial(_ml_scan_kernel, chunk=chunk, pad=pad, has_init=has_init, has_ybuf=ybuf is not None),
        grid=(nb, nh, nc),
        in_specs=in_specs,
        out_specs=[pl.BlockSpec((chunk, dh), lambda b, h, c: (row_blk0 + b * nc + c, h)),
                   pl.BlockSpec((1, 1, dh, dh), lambda b, h, c: (b, h, 0, 0)),
                   pl.BlockSpec((1, 1, 1, dh), lambda b, h, c: (b, h, 0, 0)),
                   pl.BlockSpec((1, 1, 1, LANES), lambda b, h, c: (b, h, 0, 0))],
        out_shape=[jax.ShapeDtypeStruct((nt, inner), F32),
                   jax.ShapeDtypeStruct((nb, nh, dh, dh), F32),
                   jax.ShapeDtypeStruct((nb, nh, 1, dh), F32),
                   jax.ShapeDtypeStruct((nb, nh, 1, LANES), F32)],
        scratch_shapes=[pltpu.VMEM((dh, dh), F32), pltpu.VMEM((1, dh), F32), pltpu.VMEM((1, LANES), F32)],
        input_output_aliases=aliases,
        compiler_params=_cparams(("parallel", "parallel", "arbitrary")),
        name="mlstm_scan",
    )(*args)


def _expand_block_diag(w, bw):
    nblk, bs, _ = w.shape
    per = bw // bs
    wt = w.reshape(nblk // per, per, bs, bs)
    eye = jnp.eye(per, dtype=w.dtype)
    return jnp.einsum("npio,pq->npiqo", wt, eye).reshape(nblk // per, bw, bw).astype(MXU_DTYPE)


def _sample_rows(a, lead):
    r = a.shape[1]
    a = jnp.pad(a, ((0, 0), (lead, SAMPLE_ROWS - lead - r), (0, 0)))
    return a.reshape(DEC_BATCH * SAMPLE_ROWS, a.shape[2])


def _conv_tails(proj, col0, width):
    ntp = BATCH * SEQ
    p = proj[:ntp].reshape(BATCH, SEQ, -1)[:, SEQ - (CONV_W - 1):, col0:col0 + width]
    s = proj[ntp:].reshape(DEC_BATCH, SAMPLE_ROWS, -1)[:, SAMPLE_ROWS - (CONV_W - 1):, col0:col0 + width]
    return p, s


def _mlstm_mixer(hn, x, st, w_up, conv_w, conv_b, w_q, w_k, w_v, w_gates, b_gates, skip, norm_w, w_down):
    dm = _dims()
    inner, dh, ntp = dm["ml_inner"], dm["ml_dh"], dm["ntp"]
    c0, n0, m0, conv0 = st
    up = _mm(hn, w_up[None], 0)
    conv_p, conv_s = _conv_tails(up, 0, inner)

    bw = MXU_DIM if inner % MXU_DIM == 0 else LANES
    wq, wk, wv = (_expand_block_diag(w, bw) for w in (w_q, w_k, w_v))
    wg = jnp.pad(w_gates, ((0, 0), (0, LANES - w_gates.shape[1]))).astype(MXU_DTYPE)
    bg = jnp.pad(b_gates, (0, LANES - b_gates.shape[0])).reshape(1, LANES)
    cb = conv_b.reshape(1, inner)
    cache_rows = _sample_rows(conv0, SAMPLE_PAD - (CONV_W - 1))
    skip2 = skip.reshape(1, inner)
    nw = norm_w.reshape(1, inner)

    chunk = _tile(SEQ, PROMPT_CHUNK, SUBLANES)
    qp, kp, vp, xcp, gp = _ml_pre(up, None, conv_w, cb, wq, wk, wv, wg, bg, sample=False)
    y, c_p, n_p, m_p = _ml_scan(qp, kp, vp, xcp, up, gp, skip2, nw, None, None,
                                nb=BATCH, nc=SEQ // chunk, chunk=chunk, pad=0, row_blk0=0)
    qs, ks, vs, xcs, gs = _ml_pre(up, cache_rows, conv_w, cb, wq, wk, wv, wg, bg, sample=True)
    y, c_s, n_s, m_s = _ml_scan(qs, ks, vs, xcs, up, gs, skip2, nw,
                                (c0, n0.reshape(DEC_BATCH, ML_HEADS, 1, dh), m0.reshape(-1)), y,
                                nb=DEC_BATCH, nc=1, chunk=SAMPLE_ROWS, pad=SAMPLE_PAD,
                                row_blk0=ntp // SAMPLE_ROWS)
    x = _mm(y, w_down[None], 0, res=x, scale=1.0, tm_pref=256)
    prompt_state = (c_p, n_p[:, :, 0], m_p[:, :, 0, 0], conv_p)
    sample_state = (c_s, n_s[:, :, 0], m_s[:, :, 0, 0], conv_s)
    return x, prompt_state, sample_state


def _gelu_tanh(x):
    return 0.5 * x * (1.0 + jnp.tanh(0.7978845608028654 * (x + 0.044715 * x * x * x)))


def _one_minus_exp(y):
    series = -y * (1.0 + y * (0.5 + y * (1.0 / 6.0 + y * (1.0 / 24.0 + y * (1.0 / 120.0)))))
    return jnp.where(y > -0.01, series, 1.0 - jnp.exp(y))


def _rg_kernel(gate_ref, xr_ref, halo_ref, *rest, sample, seq_len):
    rest = list(rest)
    if sample:
        cache_ref, h0_ref = rest.pop(0), rest.pop(0)
    (cw_ref, cb_ref, wa_ref, ba_ref, wx_ref, bx_ref, lam_ref, y_ref, h_ref, win_ref, carry_ref) = rest
    tm = xr_ref.shape[0]
    i = pl.program_id(2)
    xr = xr_ref[...]
    row = lax.broadcasted_iota(jnp.int32, (tm, 1), 0)
    if sample:
        tok = _token_rows(tm)
        xr = jnp.where(tok, xr, cache_ref[...])
        halo = jnp.zeros(halo_ref.shape, F32)
    else:
        halo = jnp.where(i == 0, 0.0, halo_ref[...])
    xc = _causal_conv(xr, halo, win_ref, cw_ref, cb_ref)
    r = jax.nn.sigmoid(_dot(xc, wa_ref[...]) + ba_ref[...])
    gi = jax.nn.sigmoid(_dot(xc, wx_ref[...]) + bx_ref[...])
    log_a = -RG_C * r * _softplus(-lam_ref[...])
    a = jnp.exp(log_a)
    bx = jnp.sqrt(_one_minus_exp(2.0 * log_a)) * (gi * xc)
    if sample:
        pos = row % SAMPLE_ROWS
        a = jnp.where(tok, a, 1.0)
        bx = jnp.where(tok, bx, jnp.where(pos == SAMPLE_PAD - 1, h0_ref[...], 0.0))
        _, hid = _linear_scan_rows(a, bx, SAMPLE_ROWS, pos)
        h_ref[...] = hid
        y_ref[...] = jnp.where(tok, hid * _gelu_tanh(gate_ref[...]), 0.0).astype(y_ref.dtype)
    else:
        @pl.when(i == 0)
        def _():
            carry_ref[...] = jnp.zeros(carry_ref.shape, F32)
        a_cum, b_cum = _linear_scan_rows(a, bx, tm, row)
        hid = a_cum * carry_ref[...] + b_cum
        last = hid[tm - 1:tm, :]
        carry_ref[...] = last
        y_ref[...] = (hid * _gelu_tanh(gate_ref[...])).astype(y_ref.dtype)

        @pl.when(i == pl.num_programs(2) - 1)
        def _():
            h_ref[0] = last


def _rg_mix(proj, cache_rows, h0_rows, cw, cb, wa, ba, wx, bx, lam, ybuf, *, sample):
    dm = _dims()
    width, blk, nh = dm["rg_width"], dm["rg_block"], RG_HEADS
    ntp, nts = dm["ntp"], dm["nts"]
    nt = proj.shape[0]
    if sample:
        tm = _tile(nts, PROMPT_CHUNK, SAMPLE_ROWS)
        nb, nc, blk0 = 1, nts // tm, ntp // tm
    else:
        tm = _tile(SEQ, PROMPT_CHUNK, SUBLANES)
        nb, nc, blk0 = BATCH, SEQ // tm, 0
    hb = tm // HALO

    def rows(b, i):
        return blk0 + b * nc + i

    in_specs = [pl.BlockSpec((tm, blk), lambda b, j, i: (rows(b, i), j)),
                pl.BlockSpec((tm, blk), lambda b, j, i: (rows(b, i), nh + j)),
                pl.BlockSpec((HALO, blk), lambda b, j, i: (jnp.maximum(rows(b, i) * hb - 1, 0), nh + j))]
    args = [proj, proj, proj]
    if sample:
        in_specs += [pl.BlockSpec((tm, blk), lambda b, j, i: (i, j))] * 2
        args += [cache_rows, h0_rows]
    vec = pl.BlockSpec((1, blk), lambda b, j, i: (0, j))
    mat = pl.BlockSpec((None, blk, blk), lambda b, j, i: (j, 0, 0))
    in_specs += [pl.BlockSpec((CONV_W, blk), lambda b, j, i: (0, j)), vec, mat, vec, mat, vec, vec]
    args += [cw, cb, wa, ba, wx, bx, lam]
    aliases = {}
    if ybuf is not None:
        in_specs.append(pl.BlockSpec(memory_space=pl.ANY))
        args.append(ybuf)
        aliases = {len(args) - 1: 0}
    if sample:
        h_spec = pl.BlockSpec((tm, blk), lambda b, j, i: (i, j))
        h_shape = jax.ShapeDtypeStruct((nts, width), F32)
    else:
        h_spec = pl.BlockSpec((1, 1, blk), lambda b, j, i: (b, 0, j))
        h_shape = jax.ShapeDtypeStruct((BATCH, 1, width), F32)

    def kern(*refs):
        refs = list(refs)
        if ybuf is not None:
            refs.pop(len(args) - 1)
        _rg_kernel(*refs, sample=sample, seq_len=SEQ)

    return pl.pallas_call(
        kern,
        grid=(nb, nh, nc),
        in_specs=in_specs,
        out_specs=[pl.BlockSpec((tm, blk), lambda b, j, i: (rows(b, i), j)), h_spec],
        out_shape=[jax.ShapeDtypeStruct((nt, width), F32), h_shape],
        scratch_shapes=[pltpu.VMEM((HALO + tm, blk), F32), pltpu.VMEM((1, blk), F32)],
        input_output_aliases=aliases,
        compiler_params=_cparams(("parallel", "parallel", "arbitrary")),
        name="rglru_mix",
    )(*args)


def _rglru_mixer(hn, x, st, w_in, conv_w, conv_b, w_a, b_a, w_x, b_x, lam, w_out):
    dm = _dims()
    width = dm["rg_width"]
    h0, conv0 = st
    proj = _mm(hn, w_in[None], 0)
    conv_p, conv_s = _conv_tails(proj, width, width)
    row = lambda a: a.reshape(1, width)
    cache_rows = _sample_rows(conv0, SAMPLE_PAD - (CONV_W - 1))
    h0_rows = _sample_rows(h0[:, None, :], SAMPLE_PAD - 1)
    consts = (conv_w, row(conv_b), w_a, row(b_a), w_x, row(b_x), row(lam))
    y, h_p = _rg_mix(proj, None, None, *consts, None, sample=False)
    y, h_all = _rg_mix(proj, cache_rows, h0_rows, *consts, y, sample=True)
    h_s = h_all.reshape(DEC_BATCH, SAMPLE_ROWS, width)[:, SAMPLE_ROWS - 1]
    x = _mm(y, w_out[None], 0, res=x, scale=1.0, tm_pref=256)
    return x, (h_p[:, 0], conv_p), (h_s, conv_s)


def _conv_silu_kernel(x_ref, halo_ref, *rest, sample, seq_len):
    rest = list(rest)
    cache_ref = rest.pop(0) if sample else None
    cw_ref, cb_ref, o_ref, win_ref = rest
    tm = x_ref.shape[0]
    x = x_ref[...]
    if sample:
        x = jnp.where(_token_rows(tm), x, cache_ref[...])
        halo = jnp.zeros(halo_ref.shape, F32)
    else:
        seq_start = (pl.program_id(0) * tm) % seq_len == 0
        halo = jnp.where(seq_start, 0.0, halo_ref[...])
    o_ref[...] = _silu(_causal_conv(x, halo, win_ref, cw_ref, cb_ref))


def _conv_silu(proj, col0, cache_rows, cw, cb, *, sample):
    dm = _dims()
    width = cw.shape[1]
    n = dm["nts"] if sample else dm["ntp"]
    tm = _tile(n if sample else SEQ, PROMPT_CHUNK, SAMPLE_ROWS)
    tc = _tile(math.gcd(col0, width), COL_TILE, LANES)
    blk0 = dm["ntp"] // tm if sample else 0
    cblk0 = col0 // tc
    hb = tm // HALO
    in_specs = [pl.BlockSpec((tm, tc), lambda i, j: (blk0 + i, cblk0 + j)),
                pl.BlockSpec((HALO, tc), lambda i, j: (jnp.maximum((blk0 + i) * hb - 1, 0), cblk0 + j))]
    args = [proj, proj]
    if sample:
        in_specs.append(pl.BlockSpec((tm, tc), lambda i, j: (i, j)))
        args.append(cache_rows)
    in_specs += [pl.BlockSpec((CONV_W, tc), lambda i, j: (0, j)), pl.BlockSpec((1, tc), lambda i, j: (0, j))]
    args += [cw, cb]
    return pl.pallas_call(
        functools.partial(_conv_silu_kernel, sample=sample, seq_len=SEQ),
        grid=(n // tm, width // tc),
        in_specs=in_specs,
        out_specs=pl.BlockSpec((tm, tc), lambda i, j: (i, j)),
        out_shape=jax.ShapeDtypeStruct((n, width), F32),
        scratch_shapes=[pltpu.VMEM((HALO + tm, tc), F32)],
        compiler_params=_cparams(("parallel", "parallel")),
        name="conv_silu",
    )(*args)


def _ssd_scan_kernel(xs_ref, bm_ref, cm_ref, z_ref, dt_ref, dtb_ref, alog_ref, dskip_ref, nw_ref, *rest,
                     chunk, pad, has_init, has_ybuf, n_heads, rep, headdim):
    rest = list(rest)
    s0_ref = rest.pop(0) if has_init else None
    if has_ybuf:
        rest.pop(0)
    y_ref, s_out_ref, s_ref = rest
    g = pl.program_id(1)
    c = pl.program_id(2)
    gw = rep * headdim

    @pl.when(c == 0)
    def _():
        s_ref[...] = s0_ref[0, 0] if has_init else jnp.zeros(s_ref.shape, F32)

    t_col = lax.broadcasted_iota(jnp.int32, (chunk, 1), 0)
    valid_col = t_col >= pad
    lane = lax.broadcasted_iota(jnp.int32, (chunk, LANES), 1)
    head_ok = valid_col & (lane < n_heads)
    dt = jnp.where(head_ok, _softplus(dt_ref[...] + dtb_ref[...]), 0.0)
    log_a = jnp.where(head_ok, dt * -jnp.exp(alog_ref[...]), 0.0)
    bcum = _cumsum_rows(log_a)

    ci = lax.broadcasted_iota(jnp.int32, (LANES, gw), 0)
    ji = lax.broadcasted_iota(jnp.int32, (LANES, gw), 1)
    spread = (ci == g * rep + ji // headdim).astype(F32)
    dt_w = _dot_exact(dt, spread)
    b_w = _dot_exact(bcum, spread)
    ci2 = lax.broadcasted_iota(jnp.int32, (LANES, LANES), 0)
    ri2 = lax.broadcasted_iota(jnp.int32, (LANES, LANES), 1)
    b_cols = _dot_exact(bcum, ((ci2 == g * rep + ri2) & (ri2 < rep)).astype(F32))
    r8 = lax.broadcasted_iota(jnp.int32, (SUBLANES, LANES), 0)
    c8 = lax.broadcasted_iota(jnp.int32, (SUBLANES, LANES), 1)
    b_rows = _dot_nt_exact((c8 == g * rep + r8).astype(F32), bcum)

    xs = xs_ref[...]
    bm = bm_ref[...]
    cm = cm_ref[...]
    vals = xs * dt_w
    qk = _dot_nt(cm, bm)
    t_mat = lax.broadcasted_iota(jnp.int32, (chunk, chunk), 0)
    s_mat = lax.broadcasted_iota(jnp.int32, (chunk, chunk), 1)
    tril = t_mat >= s_mat
    lane_head = lax.broadcasted_iota(jnp.int32, (1, gw), 1) // headdim
    y = jnp.zeros((chunk, gw), F32)
    for r in range(rep):
        d = jnp.where(tril, b_cols[:, r:r + 1] - b_rows[r:r + 1, :], -jnp.inf)
        y = y + _dot(qk * jnp.exp(d), jnp.where(lane_head == r, vals, 0.0))

    s = s_ref[...]
    b_last = b_w[chunk - 1:chunk, :]
    y = y + jnp.exp(b_w) * _dot(cm, s)
    s_new = jnp.exp(b_last) * s + _dot_tn(bm, vals * jnp.exp(b_last - b_w))
    s_ref[...] = s_new

    y = (y + dskip_ref[...] * xs) * _silu(z_ref[...])
    yn = y * lax.rsqrt(jnp.mean(y * y, axis=-1, keepdims=True) + EPS) * nw_ref[...]
    y_ref[...] = jnp.where(valid_col, yn, 0.0).astype(y_ref.dtype)

    @pl.when(c == pl.num_programs(2) - 1)
    def _():
        s_out_ref[0, 0] = s_new


def _ssd_scan(act, proj, dtb, alog, dskip, norm_w, s0, ybuf, *, nb, nc, chunk, pad, row_blk0):
    dm = _dims()
    inner, nheads, rep = dm["ssd_inner"], dm["ssd_heads"], dm["ssd_rep"]
    assert rep == SUBLANES and nheads <= LANES and SSD_STATE == LANES
    gw = rep * SSD_HEADDIM
    ng = SSD_GROUPS
    nt = proj.shape[0]
    bm_blk0 = inner // SSD_STATE
    dt_blk = (inner + dm["ssd_conv"]) // LANES

    def loc(b, c):
        return b * nc + c

    in_specs = [pl.BlockSpec((chunk, gw), lambda b, g, c: (loc(b, c), g)),
                pl.BlockSpec((chunk, SSD_STATE), lambda b, g, c: (loc(b, c), bm_blk0 + g)),
                pl.BlockSpec((chunk, SSD_STATE), lambda b, g, c: (loc(b, c), bm_blk0 + ng + g)),
                pl.BlockSpec((chunk, gw), lambda b, g, c: (row_blk0 + loc(b, c), g)),
                pl.BlockSpec((chunk, LANES), lambda b, g, c: (row_blk0 + loc(b, c), dt_blk)),
                pl.BlockSpec((1, LANES), lambda b, g, c: (0, 0)),
                pl.BlockSpec((1, LANES), lambda b, g, c: (0, 0)),
                pl.BlockSpec((1, gw), lambda b, g, c: (0, g)),
                pl.BlockSpec((1, gw), lambda b, g, c: (0, g))]
    args = [act, act, act, proj, proj, dtb, alog, dskip, norm_w]
    aliases = {}
    if s0 is not None:
        in_specs.append(pl.BlockSpec((1, 1, SSD_STATE, gw), lambda b, g, c: (b, g, 0, 0)))
        args.append(s0)
    if ybuf is not None:
        in_specs.append(pl.BlockSpec(memory_space=pl.ANY))
        args.append(ybuf)
        aliases = {len(args) - 1: 0}
    return pl.pallas_call(
        functools.partial(_ssd_scan_kernel, chunk=chunk, pad=pad, has_init=s0 is not None,
                          has_ybuf=ybuf is not None, n_heads=nheads, rep=rep, headdim=SSD_HEADDIM),
        grid=(nb, ng, nc),
        in_specs=in_specs,
        out_specs=[pl.BlockSpec((chunk, gw), lambda b, g, c: (row_blk0 + loc(b, c), g)),
                   pl.BlockSpec((1, 1, SSD_STATE, gw), lambda b, g, c: (b, g, 0, 0))],
        out_shape=[jax.ShapeDtypeStruct((nt, inner), F32),
                   jax.ShapeDtypeStruct((nb, ng, SSD_STATE, gw), F32)],
        scratch_shapes=[pltpu.VMEM((SSD_STATE, gw), F32)],
        input_output_aliases=aliases,
        compiler_params=_cparams(("parallel", "parallel", "arbitrary")),
        name="ssd_scan",
    )(*args)


def _ssd_state_to_wide(s, nb):
    rep = s.shape[1] // SSD_GROUPS
    s = s.reshape(nb, SSD_GROUPS, rep, SSD_STATE, SSD_HEADDIM)
    return jnp.transpose(s, (0, 1, 3, 2, 4)).reshape(nb, SSD_GROUPS, SSD_STATE, rep * SSD_HEADDIM)


def _ssd_state_from_wide(s, nb):
    rep = s.shape[3] // SSD_HEADDIM
    s = s.reshape(nb, SSD_GROUPS, SSD_STATE, rep, SSD_HEADDIM)
    return jnp.transpose(s, (0, 1, 3, 2, 4)).reshape(nb, SSD_GROUPS * rep, SSD_STATE, SSD_HEADDIM)


def _ssd_mixer(hn, x, st, w_in, conv_w, conv_b, dt_bias, a_log, d_skip, norm_w, w_out):
    dm = _dims()
    inner, cdim, nheads, ntp = dm["ssd_inner"], dm["ssd_conv"], dm["ssd_heads"], dm["ntp"]
    s0, conv0 = st
    proj = _mm(hn, w_in[None], 0)
    conv_p, conv_s = _conv_tails(proj, inner, cdim)
    cb = conv_b.reshape(1, cdim)
    cache_rows = _sample_rows(conv0, SAMPLE_PAD - (CONV_W - 1))
    padl = lambda a: jnp.pad(a, (0, LANES - nheads)).reshape(1, LANES)
    dtb, alog = padl(dt_bias), padl(a_log)
    dskip = jnp.repeat(d_skip, SSD_HEADDIM).reshape(1, inner)
    nw = norm_w.reshape(1, inner)

    chunk = _tile(SEQ, PROMPT_CHUNK, SUBLANES)
    act_p = _conv_silu(proj, inner, None, conv_w, cb, sample=False)
    y, s_p = _ssd_scan(act_p, proj, dtb, alog, dskip, nw, None, None,
                       nb=BATCH, nc=SEQ // chunk, chunk=chunk, pad=0, row_blk0=0)
    act_s = _conv_silu(proj, inner, cache_rows, conv_w, cb, sample=True)
    y, s_s = _ssd_scan(act_s, proj, dtb, alog, dskip, nw, _ssd_state_to_wide(s0, DEC_BATCH), y,
                       nb=DEC_BATCH, nc=1, chunk=SAMPLE_ROWS, pad=SAMPLE_PAD, row_blk0=ntp // SAMPLE_ROWS)
    x = _mm(y, w_out[None], 0, res=x, scale=1.0, tm_pref=256)
    return x, (_ssd_state_from_wide(s_p, BATCH), conv_p), (_ssd_state_from_wide(s_s, DEC_BATCH), conv_s)


def kernel(x_prompt, x_sample, state_ret, state_mlstm_c, state_mlstm_n, state_mlstm_m, cache_mlstm_conv,
           state_rglru_h, cache_rglru_conv, state_ssd, cache_ssd_conv,
           norm_ffn1, w_ffn1_up, w_ffn1_down, norm_mix, norm_ffn2, w_ffn2_up, w_ffn2_down, norm_final,
           ret_w_in, ret_norm, ret_w_out,
           ml_w_up, ml_conv_w, ml_conv_b, ml_w_q, ml_w_k, ml_w_v, ml_w_gates, ml_b_gates, ml_skip, ml_norm, ml_w_down,
           rg_w_in, rg_conv_w, rg_conv_b, rg_w_a, rg_b_a, rg_w_x, rg_b_x, rg_lambda, rg_w_out,
           ssd_w_in, ssd_conv_w, ssd_conv_b, ssd_dt_bias, ssd_a_log, ssd_d, ssd_norm, ssd_w_out):
    d = D_MODEL
    ntp = BATCH * SEQ
    xs = jnp.pad(x_sample, ((0, 0), (SAMPLE_PAD, 0), (0, 0))).reshape(DEC_BATCH * SAMPLE_ROWS, d)
    x = jnp.concatenate([x_prompt.reshape(ntp, d), xs], axis=0)

    n1 = norm_ffn1.reshape(DEPTH, 1, d)
    nm = norm_mix.reshape(DEPTH, 1, d)
    n2 = norm_ffn2.reshape(DEPTH, 1, d)
    states_p = {}
    states_s = {}
    for i in range(DEPTH):
        x = _ffn(x, n1, w_ffn1_up, w_ffn1_down, i)
        hn = _rmsnorm(x, nm, i, MXU_DTYPE)
        kind = i % N_MIXERS
        if kind == 0:
            x, states_p["ret"], states_s["ret"] = _retention_mixer(hn, x, state_ret, ret_w_in, ret_norm, ret_w_out)
        elif kind == 1:
            x, states_p["ml"], states_s["ml"] = _mlstm_mixer(
                hn, x, (state_mlstm_c, state_mlstm_n, state_mlstm_m, cache_mlstm_conv),
                ml_w_up, ml_conv_w, ml_conv_b, ml_w_q, ml_w_k, ml_w_v, ml_w_gates, ml_b_gates, ml_skip, ml_norm,
                ml_w_down)
        elif kind == 2:
            x, states_p["rg"], states_s["rg"] = _rglru_mixer(
                hn, x, (state_rglru_h, cache_rglru_conv), rg_w_in, rg_conv_w, rg_conv_b, rg_w_a, rg_b_a,
                rg_w_x, rg_b_x, rg_lambda, rg_w_out)
        else:
            x, states_p["ssd"], states_s["ssd"] = _ssd_mixer(
                hn, x, (state_ssd, cache_ssd_conv), ssd_w_in, ssd_conv_w, ssd_conv_b, ssd_dt_bias, ssd_a_log,
                ssd_d, ssd_norm, ssd_w_out)
        x = _ffn(x, n2, w_ffn2_up, w_ffn2_down, i)
    y = _rmsnorm(x, norm_final.reshape(1, 1, d), 0, F32)
    y_prompt = y[:ntp].reshape(BATCH, SEQ, d)
    y_sample = y[ntp:].reshape(DEC_BATCH, SAMPLE_ROWS, d)[:, SAMPLE_PAD:]

    def flat(st):
        return (st["ret"],) + tuple(st["ml"]) + tuple(st["rg"]) + tuple(st["ssd"])

    return (y_prompt, y_sample) + flat(states_p) + flat(states_s)
```

```python
import functools
import math

import jax
import jax.numpy as jnp
from jax import lax
from jax.experimental import pallas as pl
from jax.experimental.pallas import tpu as pltpu

D_MODEL = 2048
BATCH = 4
SEQ = 2048
DEPTH = 4
DEC_BATCH = 128
DEC_SEQ = 4
PAST_LEN = 16384
N_MIXERS = 4
D_FF = 5632
EPS = 1e-6
CONV_W = 4

RET_HEADS = 8
ROPE_BASE = 10000.0
ML_HEADS = 4
ML_QKV_BLOCK = 4
RG_HEADS = 10
RG_C = 8.0
SSD_HEADDIM = 64
SSD_GROUPS = 8
SSD_STATE = 128

F32 = jnp.float32
MXU_DTYPE = jnp.bfloat16
HIGHEST = lax.Precision.HIGHEST

SUBLANES = 8
LANES = 128
MXU_DIM = 256
VMEM_LIMIT = 56 * 1024 * 1024

SAMPLE_ROWS = SUBLANES
SAMPLE_PAD = SAMPLE_ROWS - DEC_SEQ
HALO = SUBLANES

ROW_TILE = 512
COL_TILE = 512
WIDE_COL_TILE = 1024
SAMPLE_SEQS_PER_STEP = 4
PROMPT_CHUNK = 256
PRE_ROW_TILE = 128


def _dims():
    d = D_MODEL
    ret_dk = d // RET_HEADS
    ml_inner = 2 * d
    rg_width = d * 5 // 4
    ssd_inner = 2 * d
    ssd_heads = ssd_inner // SSD_HEADDIM
    return dict(
        ret_dk=ret_dk, ret_dv=2 * ret_dk,
        ml_inner=ml_inner, ml_dh=ml_inner // ML_HEADS,
        rg_width=rg_width, rg_block=rg_width // RG_HEADS,
        ssd_inner=ssd_inner, ssd_heads=ssd_heads, ssd_rep=ssd_heads // SSD_GROUPS,
        ssd_conv=ssd_inner + 2 * SSD_GROUPS * SSD_STATE,
        ntp=BATCH * SEQ, nts=DEC_BATCH * SAMPLE_ROWS,
    )


def _tile(n, pref, mult):
    t = min(pref, n) // mult * mult
    while t > mult and n % t:
        t -= mult
    assert t >= mult and n % t == 0, (n, pref, mult)
    return t


def _cparams(sem):
    return pltpu.CompilerParams(dimension_semantics=sem, vmem_limit_bytes=VMEM_LIMIT)


def _silu(x):
    return x * jax.nn.sigmoid(x)


def _softplus(x):
    return jnp.maximum(x, 0.0) + jnp.log1p(jnp.exp(-jnp.abs(x)))


def _dot(a, b):
    return jnp.dot(a.astype(MXU_DTYPE), b.astype(MXU_DTYPE), preferred_element_type=F32)


def _dot_nt(a, b):
    return lax.dot_general(a.astype(MXU_DTYPE), b.astype(MXU_DTYPE), (((1,), (1,)), ((), ())),
                           preferred_element_type=F32)


def _dot_tn(a, b):
    return lax.dot_general(a.astype(MXU_DTYPE), b.astype(MXU_DTYPE), (((0,), (0,)), ((), ())),
                           preferred_element_type=F32)


def _dot_exact(a, b):
    return jnp.dot(a, b, preferred_element_type=F32, precision=HIGHEST)


def _dot_nt_exact(a, b):
    return lax.dot_general(a, b, (((1,), (1,)), ((), ())), preferred_element_type=F32, precision=HIGHEST)


def _cumsum_rows(x):
    n = x.shape[0]
    row = lax.broadcasted_iota(jnp.int32, x.shape, 0)
    sh = 1
    while sh < n:
        x = x + jnp.where(row >= sh, pltpu.roll(x, sh, 0), 0.0)
        sh *= 2
    return x


def _linear_scan_rows(a, b, span, pos):
    sh = 1
    while sh < span:
        ok = pos >= sh
        a_prev = jnp.where(ok, pltpu.roll(a, sh, 0), 1.0)
        b_prev = jnp.where(ok, pltpu.roll(b, sh, 0), 0.0)
        b = b + a * b_prev
        a = a * a_prev
        sh *= 2
    return a, b


def _causal_conv(x, halo, win_ref, w_ref, b_ref):
    tm = x.shape[0]
    win_ref[0:HALO, :] = halo
    win_ref[HALO:HALO + tm, :] = x
    y = b_ref[...] + x * w_ref[CONV_W - 1:CONV_W, :]
    for j in range(1, CONV_W):
        y = y + win_ref[pl.ds(HALO - j, tm), :] * w_ref[CONV_W - 1 - j:CONV_W - j, :]
    return y


def _token_rows(n):
    row = lax.broadcasted_iota(jnp.int32, (n, 1), 0)
    return (row % SAMPLE_ROWS) >= SAMPLE_PAD


def _rmsnorm_kernel(x_ref, w_ref, o_ref):
    x = x_ref[...]
    ms = jnp.mean(x * x, axis=-1, keepdims=True)
    o_ref[...] = (x * lax.rsqrt(ms + EPS) * w_ref[...]).astype(o_ref.dtype)


def _rmsnorm(x, w, layer, out_dtype):
    m, d = x.shape
    tm = _tile(m, ROW_TILE, 16)
    return pl.pallas_call(
        _rmsnorm_kernel,
        grid=(m // tm,),
        in_specs=[pl.BlockSpec((tm, d), lambda i: (i, 0)),
                  pl.BlockSpec((None, 1, d), lambda i: (layer, 0, 0))],
        out_specs=pl.BlockSpec((tm, d), lambda i: (i, 0)),
        out_shape=jax.ShapeDtypeStruct((m, d), out_dtype),
        compiler_params=_cparams(("parallel",)),
        name="rmsnorm",
    )(x, w)


def _mm_kernel(a_ref, w_ref, *rest, scale, has_res):
    if has_res:
        r_ref, o_ref, wb_ref = rest
    else:
        o_ref, wb_ref = rest

    @pl.when(pl.program_id(1) == 0)
    def _():
        wb_ref[...] = w_ref[...].astype(MXU_DTYPE)

    acc = jnp.dot(a_ref[...].astype(MXU_DTYPE), wb_ref[...], preferred_element_type=F32)
    if has_res:
        acc = r_ref[...] + scale * acc
    o_ref[...] = acc.astype(o_ref.dtype)


def _mm(a, w, layer, *, res=None, scale=1.0, tm_pref=ROW_TILE, tn_pref=WIDE_COL_TILE):
    m, k = a.shape
    n = w.shape[2]
    tm = _tile(m, tm_pref, 16)
    tn = min(tn_pref, n)
    nj = pl.cdiv(n, tn)
    w_mode = dict(pipeline_mode=pl.Buffered(1)) if res is not None else {}
    in_specs = [pl.BlockSpec((tm, k), lambda j, i: (i, 0)),
                pl.BlockSpec((None, k, tn), lambda j, i: (layer, 0, j), **w_mode)]
    args = [a, w]
    aliases = {}
    if res is not None:
        assert n % tn == 0 and res.shape == (m, n)
        in_specs.append(pl.BlockSpec((tm, tn), lambda j, i: (i, j)))
        args.append(res)
        aliases = {2: 0}
    return pl.pallas_call(
        functools.partial(_mm_kernel, scale=scale, has_res=res is not None),
        grid=(nj, m // tm),
        in_specs=in_specs,
        out_specs=pl.BlockSpec((tm, tn), lambda j, i: (i, j)),
        out_shape=jax.ShapeDtypeStruct((m, nj * tn), F32),
        scratch_shapes=[pltpu.VMEM((k, tn), MXU_DTYPE)],
        input_output_aliases=aliases,
        compiler_params=_cparams(("parallel", "arbitrary")),
        name="matmul",
    )(*args)


def _swiglu_up_kernel(a_ref, wg_ref, wu_ref, o_ref, wgb_ref, wub_ref):
    @pl.when(pl.program_id(1) == 0)
    def _():
        wgb_ref[...] = wg_ref[...].astype(MXU_DTYPE)
        wub_ref[...] = wu_ref[...].astype(MXU_DTYPE)

    a = a_ref[...]
    gate = jnp.dot(a, wgb_ref[...], preferred_element_type=F32)
    up = jnp.dot(a, wub_ref[...], preferred_element_type=F32)
    o_ref[...] = (_silu(gate) * up).astype(o_ref.dtype)


def _swiglu_up(a, w_up, layer):
    m, k = a.shape
    ff = w_up.shape[2] // 2
    tm = _tile(m, ROW_TILE, 16)
    tn = _tile(ff, COL_TILE, LANES)
    nj = ff // tn
    return pl.pallas_call(
        _swiglu_up_kernel,
        grid=(nj, m // tm),
        in_specs=[pl.BlockSpec((tm, k), lambda j, i: (i, 0)),
                  pl.BlockSpec((None, k, tn), lambda j, i: (layer, 0, j)),
                  pl.BlockSpec((None, k, tn), lambda j, i: (layer, 0, nj + j))],
        out_specs=pl.BlockSpec((tm, tn), lambda j, i: (i, j)),
        out_shape=jax.ShapeDtypeStruct((m, ff), MXU_DTYPE),
        scratch_shapes=[pltpu.VMEM((k, tn), MXU_DTYPE), pltpu.VMEM((k, tn), MXU_DTYPE)],
        compiler_params=_cparams(("parallel", "arbitrary")),
        name="swiglu_up",
    )(a, w_up, w_up)


def _ffn(x, norm_w, w_up, w_down, layer):
    h = _rmsnorm(x, norm_w, layer, MXU_DTYPE)
    act = _swiglu_up(h, w_up, layer)
    return _mm(act, w_down, layer, res=x, scale=0.5, tm_pref=256)


def _ret_scan_kernel(lg_ref, q_ref, k_ref, v_ref, g_ref, cos_ref, sin_ref, nw_ref, *rest,
                     chunk, pad, has_init, has_ybuf, k_scale):
    rest = list(rest)
    s0_ref = rest.pop(0) if has_init else None
    if has_ybuf:
        rest.pop(0)
    y_ref, s_out_ref, s_ref = rest
    h = pl.program_id(1)
    c = pl.program_id(2)
    nvalid = chunk - pad
    nseq = s_out_ref.shape[0]
    lam = lg_ref[h]
    half = cos_ref.shape[1]

    t_col = lax.broadcasted_iota(jnp.int32, (chunk, 1), 0)
    cnt_col = jnp.maximum(t_col - pad + 1, 0).astype(F32)
    t_mat = lax.broadcasted_iota(jnp.int32, (chunk, chunk), 0)
    s_mat = lax.broadcasted_iota(jnp.int32, (chunk, chunk), 1)
    cnt_t = jnp.maximum(t_mat - pad + 1, 0).astype(F32)
    cnt_s = jnp.maximum(s_mat - pad + 1, 0).astype(F32)
    w = jnp.where((t_mat >= s_mat) & (s_mat >= pad), jnp.exp(lam * (cnt_t - cnt_s)), 0.0)
    ws = jnp.where(t_col >= pad, jnp.exp(lam * (nvalid - cnt_col)), 0.0)
    w_in = jnp.exp(lam * cnt_col)
    w_state = jnp.exp(lam * nvalid)

    def one_seq(i, s):
        r = slice(i * chunk, (i + 1) * chunk)
        cos = cos_ref[r, :]
        sin = sin_ref[r, :]

        def rot(x):
            x1 = x[:, :half]
            x2 = x[:, half:]
            return jnp.concatenate([x1 * cos - x2 * sin, x1 * sin + x2 * cos], axis=-1)

        q = rot(q_ref[r, :])
        k = rot(k_ref[r, :]) * k_scale
        v = v_ref[r, :]
        y = _dot(_dot_nt(q, k) * w, v) + w_in * _dot(q, s)
        s_new = w_state * s + _dot_tn(k * ws, v)
        mu = jnp.mean(y, axis=-1, keepdims=True)
        yc = y - mu
        yn = yc * lax.rsqrt(jnp.mean(yc * yc, axis=-1, keepdims=True) + EPS) * nw_ref[...]
        out = _silu(g_ref[r, :]) * yn
        y_ref[r, :] = jnp.where(t_col >= pad, out, 0.0).astype(y_ref.dtype)
        return s_new

    if has_init:
        for i in range(nseq):
            s_out_ref[i, 0] = one_seq(i, s0_ref[i, 0])
    else:
        @pl.when(c == 0)
        def _():
            s_ref[...] = jnp.zeros(s_ref.shape, F32)

        s_new = one_seq(0, s_ref[...])
        s_ref[...] = s_new

        @pl.when(c == pl.num_programs(2) - 1)
        def _():
            s_out_ref[0, 0] = s_new


def _ret_scan(proj, cos, sin, log_gamma, norm_w, s0, ybuf, *, nb, nc, chunk, pad, row_blk0, nseq=1):
    dm = _dims()
    dk, dv, nh = dm["ret_dk"], dm["ret_dv"], RET_HEADS
    nt = proj.shape[0]
    assert nseq == 1 or (nc == 1 and s0 is not None)
    rb = nseq * chunk

    def rows(b, c):
        return row_blk0 + b * nc + c

    in_specs = [pl.BlockSpec(memory_space=pltpu.SMEM),
                pl.BlockSpec((rb, dk), lambda b, h, c: (rows(b, c), h)),
                pl.BlockSpec((rb, dk), lambda b, h, c: (rows(b, c), nh + h)),
                pl.BlockSpec((rb, dv), lambda b, h, c: (rows(b, c), nh + h)),
                pl.BlockSpec((rb, dv), lambda b, h, c: (rows(b, c), 2 * nh + h)),
                pl.BlockSpec((rb, dk // 2), lambda b, h, c: (rows(b, c), 0)),
                pl.BlockSpec((rb, dk // 2), lambda b, h, c: (rows(b, c), 0)),
                pl.BlockSpec((1, dv), lambda b, h, c: (0, h))]
    args = [log_gamma, proj, proj, proj, proj, cos, sin, norm_w]
    aliases = {}
    if s0 is not None:
        in_specs.append(pl.BlockSpec((nseq, 1, dk, dv), lambda b, h, c: (b, h, 0, 0)))
        args.append(s0)
    if ybuf is not None:
        in_specs.append(pl.BlockSpec(memory_space=pl.ANY))
        args.append(ybuf)
        aliases = {len(args) - 1: 0}
    return pl.pallas_call(
        functools.partial(_ret_scan_kernel, chunk=chunk, pad=pad, has_init=s0 is not None,
                          has_ybuf=ybuf is not None, k_scale=dk ** -0.5),
        grid=(nb, nh, nc),
        in_specs=in_specs,
        out_specs=[pl.BlockSpec((rb, dv), lambda b, h, c: (rows(b, c), h)),
                   pl.BlockSpec((nseq, 1, dk, dv), lambda b, h, c: (b, h, 0, 0))],
        out_shape=[jax.ShapeDtypeStruct((nt, nh * dv), F32),
                   jax.ShapeDtypeStruct((nb * nseq, nh, dk, dv), F32)],
        scratch_shapes=[pltpu.VMEM((dk, dv), F32)],
        input_output_aliases=aliases,
        compiler_params=_cparams(("parallel", "parallel", "arbitrary")),
        name="retention_scan",
    )(*args)


def _retention_mixer(hn, x, state_ret, w_in, norm_w, w_out):
    dm = _dims()
    ntp, nts = dm["ntp"], dm["nts"]
    dk = dm["ret_dk"]
    proj = _mm(hn, w_in[None], 0)

    half = dk // 2
    inv = ROPE_BASE ** (-jnp.arange(half, dtype=F32) / half)
    pos_p = jnp.tile(jnp.arange(SEQ), BATCH)
    pos_s = jnp.tile(PAST_LEN - SAMPLE_PAD + jnp.arange(SAMPLE_ROWS), DEC_BATCH)
    ang = jnp.concatenate([pos_p, pos_s]).astype(F32)[:, None] * inv
    cos, sin = jnp.cos(ang), jnp.sin(ang)
    log_gamma = jnp.log1p(-(2.0 ** (-5.0 - jnp.arange(RET_HEADS, dtype=F32))))
    nw = norm_w.reshape(1, -1)

    chunk = _tile(SEQ, PROMPT_CHUNK, SUBLANES)
    y, s_p = _ret_scan(proj, cos, sin, log_gamma, nw, None, None,
                       nb=BATCH, nc=SEQ // chunk, chunk=chunk, pad=0, row_blk0=0)
    nseq = _tile(DEC_BATCH, SAMPLE_SEQS_PER_STEP, 1)
    y, s_s = _ret_scan(proj, cos, sin, log_gamma, nw, state_ret, y,
                       nb=DEC_BATCH // nseq, nc=1, chunk=SAMPLE_ROWS, pad=SAMPLE_PAD,
                       row_blk0=ntp // (nseq * SAMPLE_ROWS), nseq=nseq)
    x = _mm(y, w_out[None], 0, res=x, scale=1.0, tm_pref=256)
    return x, s_p, s_s


def _ml_pre_kernel(xm_ref, halo_ref, *rest, sample, seq_len, k_scale, nblk, bw):
    rest = list(rest)
    cache_ref = rest.pop(0) if sample else None
    (cw_ref, cb_ref, wq_ref, wk_ref, wv_ref, wg_ref, bg_ref,
     q_ref, k_ref, v_ref, xc_ref, gates_ref, win_ref) = rest
    tm = xm_ref.shape[0]
    xm = xm_ref[...]
    if sample:
        xm = jnp.where(_token_rows(tm), xm, cache_ref[...])
        halo = jnp.zeros(halo_ref.shape, F32)
    else:
        seq_start = (pl.program_id(0) * tm) % seq_len == 0
        halo = jnp.where(seq_start, 0.0, halo_ref[...])
    xc = _silu(_causal_conv(xm, halo, win_ref, cw_ref, cb_ref))
    xc_ref[...] = xc

    for j in range(nblk):
        sl = slice(j * bw, (j + 1) * bw)
        q_ref[:, sl] = _dot(xc[:, sl], wq_ref[j])
        k_ref[:, sl] = _dot(xc[:, sl], wk_ref[j])
        v_ref[:, sl] = _dot(xm[:, sl], wv_ref[j])
    inner = q_ref.shape[1]
    g = (_dot(q_ref[...], wg_ref[0:inner, :]) + _dot(k_ref[...], wg_ref[inner:2 * inner, :])
         + _dot(v_ref[...], wg_ref[2 * inner:3 * inner, :]) + bg_ref[...])
    k_ref[...] = k_ref[...] * k_scale
    lane = lax.broadcasted_iota(jnp.int32, g.shape, 1)
    log_f = jnp.minimum(g, 0.0) - jnp.log1p(jnp.exp(-jnp.abs(g)))
    gates_ref[...] = jnp.where(lane < ML_HEADS, g, jnp.where(lane < 2 * ML_HEADS, log_f, 0.0))


def _ml_pre(up, cache_rows, cw, cb, wq, wk, wv, wg, bg, *, sample):
    dm = _dims()
    inner, dh = dm["ml_inner"], dm["ml_dh"]
    n = dm["nts"] if sample else dm["ntp"]
    tm = _tile(n if sample else SEQ, PRE_ROW_TILE, SAMPLE_ROWS)
    blk0 = dm["ntp"] // tm if sample else 0
    hb = tm // HALO
    nblk, bw = wq.shape[0], wq.shape[1]

    in_specs = [pl.BlockSpec((tm, inner), lambda i: (blk0 + i, 0)),
                pl.BlockSpec((HALO, inner), lambda i: (jnp.maximum((blk0 + i) * hb - 1, 0), 0))]
    args = [up, up]
    if sample:
        in_specs.append(pl.BlockSpec((tm, inner), lambda i: (i, 0)))
        args.append(cache_rows)
    const2 = lambda i: (0, 0)
    const3 = lambda i: (0, 0, 0)
    in_specs += [pl.BlockSpec(cw.shape, const2), pl.BlockSpec(cb.shape, const2),
                 pl.BlockSpec(wq.shape, const3), pl.BlockSpec(wk.shape, const3), pl.BlockSpec(wv.shape, const3),
                 pl.BlockSpec(wg.shape, const2), pl.BlockSpec(bg.shape, const2)]
    args += [cw, cb, wq, wk, wv, wg, bg]
    row_spec = pl.BlockSpec((tm, inner), lambda i: (i, 0))
    return pl.pallas_call(
        functools.partial(_ml_pre_kernel, sample=sample, seq_len=SEQ, k_scale=dh ** -0.5, nblk=nblk, bw=bw),
        grid=(n // tm,),
        in_specs=in_specs,
        out_specs=[row_spec, row_spec, row_spec, row_spec, pl.BlockSpec((tm, LANES), lambda i: (i, 0))],
        out_shape=[jax.ShapeDtypeStruct((n, inner), F32)] * 4 + [jax.ShapeDtypeStruct((n, LANES), F32)],
        scratch_shapes=[pltpu.VMEM((HALO + tm, inner), F32)],
        compiler_params=_cparams(("arbitrary",)),
        name="mlstm_pre",
    )(*args)


def _ml_scan_kernel(m0_ref, q_ref, k_ref, v_ref, xc_ref, z_ref, gates_ref, skip_ref, nw_ref, *rest,
                    chunk, pad, has_init, has_ybuf):
    rest = list(rest)
    if has_init:
        c0_ref, n0_ref = rest.pop(0), rest.pop(0)
    if has_ybuf:
        rest.pop(0)
    y_ref, c_out_ref, n_out_ref, m_out_ref, c_ref, n_ref, m_ref = rest
    b = pl.program_id(0)
    h = pl.program_id(1)
    c = pl.program_id(2)

    @pl.when(c == 0)
    def _():
        if has_init:
            c_ref[...] = c0_ref[0, 0]
            n_ref[...] = n0_ref[0, 0]
            m_ref[...] = jnp.full(m_ref.shape, m0_ref[b * ML_HEADS + h], F32)
        else:
            c_ref[...] = jnp.zeros(c_ref.shape, F32)
            n_ref[...] = jnp.zeros(n_ref.shape, F32)
            m_ref[...] = jnp.zeros(m_ref.shape, F32)

    q = q_ref[...]
    k = k_ref[...]
    v = v_ref[...]
    t_col = lax.broadcasted_iota(jnp.int32, (chunk, 1), 0)
    valid_col = t_col >= pad

    gates = jnp.where(valid_col, gates_ref[...], 0.0)
    csum = _cumsum_rows(gates)
    lane = lax.broadcasted_iota(jnp.int32, gates.shape, 1)
    ig_col = jnp.sum(jnp.where(lane == h, gates, 0.0), axis=1, keepdims=True)
    b_col = jnp.sum(jnp.where(lane == ML_HEADS + h, csum, 0.0), axis=1, keepdims=True)
    ig_col = jnp.where(valid_col, ig_col, -jnp.inf)
    sel = (lax.broadcasted_iota(jnp.int32, (SUBLANES, LANES), 0)
           == lax.broadcasted_iota(jnp.int32, (SUBLANES, LANES), 1)).astype(F32)
    sub = lax.broadcasted_iota(jnp.int32, (SUBLANES, chunk), 0)
    ig_row = jnp.sum(jnp.where(sub == h, _dot_nt_exact(sel, gates), 0.0), axis=0, keepdims=True)
    b_row = jnp.sum(jnp.where(sub == ML_HEADS + h, _dot_nt_exact(sel, csum), 0.0), axis=0, keepdims=True)
    s_row = lax.broadcasted_iota(jnp.int32, (1, chunk), 1)
    ig_row = jnp.where(s_row >= pad, ig_row, -jnp.inf)

    t_mat = lax.broadcasted_iota(jnp.int32, (chunk, chunk), 0)
    s_mat = lax.broadcasted_iota(jnp.int32, (chunk, chunk), 1)
    d = jnp.where(t_mat >= s_mat, b_col - b_row + ig_row, -jnp.inf)
    m_prev = m_ref[...][:, 0:1]
    inter = b_col + m_prev
    m_t = jnp.maximum(inter, jnp.max(d, axis=1, keepdims=True))
    sc = _dot_nt(q, k) * jnp.exp(d - m_t)
    w_inter = jnp.exp(inter - m_t)
    cmat = c_ref[...]
    nvec = n_ref[...]
    num = _dot(sc, v) + w_inter * _dot(q, cmat)
    den = jnp.sum(sc, axis=1, keepdims=True) + w_inter * jnp.sum(q * nvec, axis=1, keepdims=True)
    hid = num / jnp.maximum(jnp.abs(den), jnp.exp(-m_t))

    m_new = m_t[chunk - 1:chunk, :]
    b_last = b_col[chunk - 1:chunk, :]
    ws = jnp.exp(b_last - b_col + ig_col - m_new)
    decay = jnp.exp(b_last + m_prev - m_new)
    kw = k * ws
    c_new = decay * cmat + _dot_tn(kw, v)
    n_new = decay * nvec + jnp.sum(kw, axis=0, keepdims=True)
    c_ref[...] = c_new
    n_ref[...] = n_new
    m_ref[...] = jnp.broadcast_to(m_new, m_ref.shape)

    mu = jnp.mean(hid, axis=-1, keepdims=True)
    hc = hid - mu
    hn = hc * lax.rsqrt(jnp.mean(hc * hc, axis=-1, keepdims=True) + EPS) * nw_ref[...]
    out = (hn + skip_ref[...] * xc_ref[...]) * _silu(z_ref[...])
    y_ref[...] = jnp.where(valid_col, out, 0.0).astype(y_ref.dtype)

    @pl.when(c == pl.num_programs(2) - 1)
    def _():
        c_out_ref[0, 0] = c_new
        n_out_ref[0, 0] = n_new
        m_out_ref[0, 0] = jnp.broadcast_to(m_new, (1, LANES))


def _ml_scan(q, k, v, xc, up, gates, skip, norm_w, init, ybuf, *, nb, nc, chunk, pad, row_blk0):
    dm = _dims()
    dh, nh, inner = dm["ml_dh"], ML_HEADS, dm["ml_inner"]
    nt = up.shape[0]

    def loc(b, h, c):
        return (b * nc + c, h)

    head_spec = pl.BlockSpec((chunk, dh), loc)
    in_specs = [pl.BlockSpec(memory_space=pltpu.SMEM),
                head_spec, head_spec, head_spec, head_spec,
                pl.BlockSpec((chunk, dh), lambda b, h, c: (row_blk0 + b * nc + c, nh + h)),
                pl.BlockSpec((chunk, LANES), lambda b, h, c: (b * nc + c, 0)),
                pl.BlockSpec((1, dh), lambda b, h, c: (0, h)),
                pl.BlockSpec((1, dh), lambda b, h, c: (0, h))]
    has_init = init is not None
    if has_init:
        c0, n0, m0 = init
    else:
        c0 = n0 = None
        m0 = jnp.zeros((1,), F32)
    args = [m0, q, k, v, xc, up, gates, skip, norm_w]
    if has_init:
        in_specs += [pl.BlockSpec((1, 1, dh, dh), lambda b, h, c: (b, h, 0, 0)),
                     pl.BlockSpec((1, 1, 1, dh), lambda b, h, c: (b, h, 0, 0))]
        args += [c0, n0]
    aliases = {}
    if ybuf is not None:
        in_specs.append(pl.BlockSpec(memory_space=pl.ANY))
        args.append(ybuf)
        aliases = {len(args) - 1: 0}
    return pl.pallas_call(
        functools.partial(_ml_scan_kernel, chunk=chunk, pad=pad, has_init=has_init, has_ybuf=ybuf is not None),
        grid=(nb, nh, nc),
        in_specs=in_specs,
        out_specs=[pl.BlockSpec((chunk, dh), lambda b, h, c: (row_blk0 + b * nc + c, h)),
                   pl.BlockSpec((1, 1, dh, dh), lambda b, h, c: (b, h, 0, 0)),
                   pl.BlockSpec((1, 1, 1, dh), lambda b, h, c: (b, h, 0, 0)),
                   pl.BlockSpec((1, 1, 1, LANES), lambda b, h, c: (b, h, 0, 0))],
        out_shape=[jax.ShapeDtypeStruct((nt, inner), F32),
                   jax.ShapeDtypeStruct((nb, nh, dh, dh), F32),
                   jax.ShapeDtypeStruct((nb, nh, 1, dh), F32),
                   jax.ShapeDtypeStruct((nb, nh, 1, LANES), F32)],
        scratch_shapes=[pltpu.VMEM((dh, dh), F32), pltpu.VMEM((1, dh), F32), pltpu.VMEM((1, LANES), F32)],
        input_output_aliases=aliases,
        compiler_params=_cparams(("parallel", "parallel", "arbitrary")),
        name="mlstm_scan",
    )(*args)


def _expand_block_diag(w, bw):
    nblk, bs, _ = w.shape
    per = bw // bs
    wt = w.reshape(nblk // per, per, bs, bs)
    eye = jnp.eye(per, dtype=w.dtype)
    return jnp.einsum("npio,pq->npiqo", wt, eye).reshape(nblk // per, bw, bw).astype(MXU_DTYPE)


def _sample_rows(a, lead):
    r = a.shape[1]
    a = jnp.pad(a, ((0, 0), (lead, SAMPLE_ROWS - lead - r), (0, 0)))
    return a.reshape(DEC_BATCH * SAMPLE_ROWS, a.shape[2])


def _conv_tails(proj, col0, width):
    ntp = BATCH * SEQ
    tail = CONV_W - 1
    p = jnp.stack([lax.slice(proj, ((b + 1) * SEQ - tail, col0), ((b + 1) * SEQ, col0 + width))
                   for b in range(BATCH)])
    s = lax.slice(proj, (ntp, col0), (proj.shape[0], col0 + width))
    s = s.reshape(DEC_BATCH, SAMPLE_ROWS, width)[:, SAMPLE_ROWS - tail:]
    return p, s


def _mlstm_mixer(hn, x, st, w_up, conv_w, conv_b, w_q, w_k, w_v, w_gates, b_gates, skip, norm_w, w_down):
    dm = _dims()
    inner, dh, ntp = dm["ml_inner"], dm["ml_dh"], dm["ntp"]
    c0, n0, m0, conv0 = st
    up = _mm(hn, w_up[None], 0)
    conv_p, conv_s = _conv_tails(up, 0, inner)

    bw = MXU_DIM if inner % MXU_DIM == 0 else LANES
    wq, wk, wv = (_expand_block_diag(w, bw) for w in (w_q, w_k, w_v))
    wg = jnp.pad(w_gates, ((0, 0), (0, LANES - w_gates.shape[1]))).astype(MXU_DTYPE)
    bg = jnp.pad(b_gates, (0, LANES - b_gates.shape[0])).reshape(1, LANES)
    cb = conv_b.reshape(1, inner)
    cache_rows = _sample_rows(conv0, SAMPLE_PAD - (CONV_W - 1))
    skip2 = skip.reshape(1, inner)
    nw = norm_w.reshape(1, inner)

    chunk = _tile(SEQ, PROMPT_CHUNK, SUBLANES)
    qp, kp, vp, xcp, gp = _ml_pre(up, None, conv_w, cb, wq, wk, wv, wg, bg, sample=False)
    y, c_p, n_p, m_p = _ml_scan(qp, kp, vp, xcp, up, gp, skip2, nw, None, None,
                                nb=BATCH, nc=SEQ // chunk, chunk=chunk, pad=0, row_blk0=0)
    qs, ks, vs, xcs, gs = _ml_pre(up, cache_rows, conv_w, cb, wq, wk, wv, wg, bg, sample=True)
    y, c_s, n_s, m_s = _ml_scan(qs, ks, vs, xcs, up, gs, skip2, nw,
                                (c0, n0.reshape(DEC_BATCH, ML_HEADS, 1, dh), m0.reshape(-1)), y,
                                nb=DEC_BATCH, nc=1, chunk=SAMPLE_ROWS, pad=SAMPLE_PAD,
                                row_blk0=ntp // SAMPLE_ROWS)
    x = _mm(y, w_down[None], 0, res=x, scale=1.0, tm_pref=256)
    prompt_state = (c_p, n_p[:, :, 0], m_p[:, :, 0, 0], conv_p)
    sample_state = (c_s, n_s[:, :, 0], m_s[:, :, 0, 0], conv_s)
    return x, prompt_state, sample_state


def _gelu_tanh(x):
    return 0.5 * x * (1.0 + jnp.tanh(0.7978845608028654 * (x + 0.044715 * x * x * x)))


def _one_minus_exp(y):
    series = -y * (1.0 + y * (0.5 + y * (1.0 / 6.0 + y * (1.0 / 24.0 + y * (1.0 / 120.0)))))
    return jnp.where(y > -0.01, series, 1.0 - jnp.exp(y))


def _rg_kernel(gate_ref, xr_ref, halo_ref, *rest, sample, seq_len):
    rest = list(rest)
    if sample:
        cache_ref, h0_ref = rest.pop(0), rest.pop(0)
    (cw_ref, cb_ref, wa_ref, ba_ref, wx_ref, bx_ref, lam_ref, y_ref, h_ref, win_ref, carry_ref) = rest
    tm = xr_ref.shape[0]
    i = pl.program_id(2)
    xr = xr_ref[...]
    row = lax.broadcasted_iota(jnp.int32, (tm, 1), 0)
    if sample:
        tok = _token_rows(tm)
        xr = jnp.where(tok, xr, cache_ref[...])
        halo = jnp.zeros(halo_ref.shape, F32)
    else:
        halo = jnp.where(i == 0, 0.0, halo_ref[...])
    xc = _causal_conv(xr, halo, win_ref, cw_ref, cb_ref)
    r = jax.nn.sigmoid(_dot(xc, wa_ref[...]) + ba_ref[...])
    gi = jax.nn.sigmoid(_dot(xc, wx_ref[...]) + bx_ref[...])
    log_a = -RG_C * r * _softplus(-lam_ref[...])
    a = jnp.exp(log_a)
    bx = jnp.sqrt(_one_minus_exp(2.0 * log_a)) * (gi * xc)
    if sample:
        pos = row % SAMPLE_ROWS
        a = jnp.where(tok, a, 1.0)
        bx = jnp.where(tok, bx, jnp.where(pos == SAMPLE_PAD - 1, h0_ref[...], 0.0))
        _, hid = _linear_scan_rows(a, bx, SAMPLE_ROWS, pos)
        h_ref[...] = hid
        y_ref[...] = jnp.where(tok, hid * _gelu_tanh(gate_ref[...]), 0.0).astype(y_ref.dtype)
    else:
        @pl.when(i == 0)
        def _():
            carry_ref[...] = jnp.zeros(carry_ref.shape, F32)
        a_cum, b_cum = _linear_scan_rows(a, bx, tm, row)
        hid = a_cum * carry_ref[...] + b_cum
        last = hid[tm - 1:tm, :]
        carry_ref[...] = last
        y_ref[...] = (hid * _gelu_tanh(gate_ref[...])).astype(y_ref.dtype)

        @pl.when(i == pl.num_programs(2) - 1)
        def _():
            h_ref[0] = last


def _rg_mix(proj, cache_rows, h0_rows, cw, cb, wa, ba, wx, bx, lam, ybuf, *, sample):
    dm = _dims()
    width, blk, nh = dm["rg_width"], dm["rg_block"], RG_HEADS
    ntp, nts = dm["ntp"], dm["nts"]
    nt = proj.shape[0]
    if sample:
        tm = _tile(nts, PROMPT_CHUNK, SAMPLE_ROWS)
        nb, nc, blk0 = 1, nts // tm, ntp // tm
    else:
        tm = _tile(SEQ, PROMPT_CHUNK, SUBLANES)
        nb, nc, blk0 = BATCH, SEQ // tm, 0
    hb = tm // HALO

    def rows(b, i):
        return blk0 + b * nc + i

    in_specs = [pl.BlockSpec((tm, blk), lambda b, j, i: (rows(b, i), j)),
                pl.BlockSpec((tm, blk), lambda b, j, i: (rows(b, i), nh + j)),
                pl.BlockSpec((HALO, blk), lambda b, j, i: (jnp.maximum(rows(b, i) * hb - 1, 0), nh + j))]
    args = [proj, proj, proj]
    if sample:
        in_specs += [pl.BlockSpec((tm, blk), lambda b, j, i: (i, j))] * 2
        args += [cache_rows, h0_rows]
    vec = pl.BlockSpec((1, blk), lambda b, j, i: (0, j))
    mat = pl.BlockSpec((None, blk, blk), lambda b, j, i: (j, 0, 0))
    in_specs += [pl.BlockSpec((CONV_W, blk), lambda b, j, i: (0, j)), vec, mat, vec, mat, vec, vec]
    args += [cw, cb, wa, ba, wx, bx, lam]
    aliases = {}
    if ybuf is not None:
        in_specs.append(pl.BlockSpec(memory_space=pl.ANY))
        args.append(ybuf)
        aliases = {len(args) - 1: 0}
    if sample:
        h_spec = pl.BlockSpec((tm, blk), lambda b, j, i: (i, j))
        h_shape = jax.ShapeDtypeStruct((nts, width), F32)
    else:
        h_spec = pl.BlockSpec((1, 1, blk), lambda b, j, i: (b, 0, j))
        h_shape = jax.ShapeDtypeStruct((BATCH, 1, width), F32)

    def kern(*refs):
        refs = list(refs)
        if ybuf is not None:
            refs.pop(len(args) - 1)
        _rg_kernel(*refs, sample=sample, seq_len=SEQ)

    return pl.pallas_call(
        kern,
        grid=(nb, nh, nc),
        in_specs=in_specs,
        out_specs=[pl.BlockSpec((tm, blk), lambda b, j, i: (rows(b, i), j)), h_spec],
        out_shape=[jax.ShapeDtypeStruct((nt, width), F32), h_shape],
        scratch_shapes=[pltpu.VMEM((HALO + tm, blk), F32), pltpu.VMEM((1, blk), F32)],
        input_output_aliases=aliases,
        compiler_params=_cparams(("parallel", "parallel", "arbitrary")),
        name="rglru_mix",
    )(*args)


def _rglru_mixer(hn, x, st, w_in, conv_w, conv_b, w_a, b_a, w_x, b_x, lam, w_out):
    dm = _dims()
    width = dm["rg_width"]
    h0, conv0 = st
    proj = _mm(hn, w_in[None], 0)
    conv_p, conv_s = _conv_tails(proj, width, width)
    row = lambda a: a.reshape(1, width)
    cache_rows = _sample_rows(conv0, SAMPLE_PAD - (CONV_W - 1))
    h0_rows = _sample_rows(h0[:, None, :], SAMPLE_PAD - 1)
    consts = (conv_w, row(conv_b), w_a, row(b_a), w_x, row(b_x), row(lam))
    y, h_p = _rg_mix(proj, None, None, *consts, None, sample=False)
    y, h_all = _rg_mix(proj, cache_rows, h0_rows, *consts, y, sample=True)
    h_s = h_all.reshape(DEC_BATCH, SAMPLE_ROWS, width)[:, SAMPLE_ROWS - 1]
    x = _mm(y, w_out[None], 0, res=x, scale=1.0, tm_pref=256)
    return x, (h_p[:, 0], conv_p), (h_s, conv_s)


def _conv_silu_kernel(x_ref, halo_ref, *rest, sample, seq_len):
    rest = list(rest)
    cache_ref = rest.pop(0) if sample else None
    cw_ref, cb_ref, o_ref, win_ref = rest
    tm = x_ref.shape[0]
    x = x_ref[...]
    if sample:
        x = jnp.where(_token_rows(tm), x, cache_ref[...])
        halo = jnp.zeros(halo_ref.shape, F32)
    else:
        seq_start = (pl.program_id(0) * tm) % seq_len == 0
        halo = jnp.where(seq_start, 0.0, halo_ref[...])
    o_ref[...] = _silu(_causal_conv(x, halo, win_ref, cw_ref, cb_ref))


def _conv_silu(proj, col0, cache_rows, cw, cb, *, sample):
    dm = _dims()
    width = cw.shape[1]
    n = dm["nts"] if sample else dm["ntp"]
    tm = _tile(n if sample else SEQ, PROMPT_CHUNK, SAMPLE_ROWS)
    tc = _tile(math.gcd(col0, width), COL_TILE, LANES)
    blk0 = dm["ntp"] // tm if sample else 0
    cblk0 = col0 // tc
    hb = tm // HALO
    in_specs = [pl.BlockSpec((tm, tc), lambda i, j: (blk0 + i, cblk0 + j)),
                pl.BlockSpec((HALO, tc), lambda i, j: (jnp.maximum((blk0 + i) * hb - 1, 0), cblk0 + j))]
    args = [proj, proj]
    if sample:
        in_specs.append(pl.BlockSpec((tm, tc), lambda i, j: (i, j)))
        args.append(cache_rows)
    in_specs += [pl.BlockSpec((CONV_W, tc), lambda i, j: (0, j)), pl.BlockSpec((1, tc), lambda i, j: (0, j))]
    args += [cw, cb]
    return pl.pallas_call(
        functools.partial(_conv_silu_kernel, sample=sample, seq_len=SEQ),
        grid=(n // tm, width // tc),
        in_specs=in_specs,
        out_specs=pl.BlockSpec((tm, tc), lambda i, j: (i, j)),
        out_shape=jax.ShapeDtypeStruct((n, width), F32),
        scratch_shapes=[pltpu.VMEM((HALO + tm, tc), F32)],
        compiler_params=_cparams(("parallel", "parallel")),
        name="conv_silu",
    )(*args)


def _ssd_scan_kernel(xs_ref, bm_ref, cm_ref, z_ref, dt_ref, dtb_ref, alog_ref, dskip_ref, nw_ref, *rest,
                     chunk, pad, has_init, has_ybuf, n_heads, rep, headdim):
    rest = list(rest)
    s0_ref = rest.pop(0) if has_init else None
    if has_ybuf:
        rest.pop(0)
    y_ref, s_out_ref, s_ref = rest
    g = pl.program_id(1)
    c = pl.program_id(2)
    gw = rep * headdim
    nseq = s_out_ref.shape[0]

    t_col = lax.broadcasted_iota(jnp.int32, (chunk, 1), 0)
    valid_col = t_col >= pad
    lane = lax.broadcasted_iota(jnp.int32, (chunk, LANES), 1)
    head_ok = valid_col & (lane < n_heads)
    ci = lax.broadcasted_iota(jnp.int32, (LANES, gw), 0)
    ji = lax.broadcasted_iota(jnp.int32, (LANES, gw), 1)
    spread = (ci == g * rep + ji // headdim).astype(F32)
    ci2 = lax.broadcasted_iota(jnp.int32, (LANES, LANES), 0)
    ri2 = lax.broadcasted_iota(jnp.int32, (LANES, LANES), 1)
    to_lanes = ((ci2 == g * rep + ri2) & (ri2 < rep)).astype(F32)
    r8 = lax.broadcasted_iota(jnp.int32, (SUBLANES, LANES), 0)
    c8 = lax.broadcasted_iota(jnp.int32, (SUBLANES, LANES), 1)
    to_sublanes = (c8 == g * rep + r8).astype(F32)
    t_mat = lax.broadcasted_iota(jnp.int32, (chunk, chunk), 0)
    s_mat = lax.broadcasted_iota(jnp.int32, (chunk, chunk), 1)
    tril = t_mat >= s_mat
    lane_head = lax.broadcasted_iota(jnp.int32, (1, gw), 1) // headdim
    neg_a = -jnp.exp(alog_ref[...])

    def one_seq(i, s):
        rows = slice(i * chunk, (i + 1) * chunk)
        dt = jnp.where(head_ok, _softplus(dt_ref[rows, :] + dtb_ref[...]), 0.0)
        log_a = jnp.where(head_ok, dt * neg_a, 0.0)
        bcum = _cumsum_rows(log_a)
        dt_w = _dot_exact(dt, spread)
        b_w = _dot_exact(bcum, spread)
        b_cols = _dot_exact(bcum, to_lanes)
        b_rows = _dot_nt_exact(to_sublanes, bcum)

        xs = xs_ref[rows, :]
        bm = bm_ref[rows, :]
        cm = cm_ref[rows, :]
        vals = xs * dt_w
        qk = _dot_nt(cm, bm)
        y = jnp.zeros((chunk, gw), F32)
        for r in range(rep):
            d = jnp.where(tril, b_cols[:, r:r + 1] - b_rows[r:r + 1, :], -jnp.inf)
            y = y + _dot(qk * jnp.exp(d), jnp.where(lane_head == r, vals, 0.0))
        b_last = b_w[chunk - 1:chunk, :]
        y = y + jnp.exp(b_w) * _dot(cm, s)
        s_new = jnp.exp(b_last) * s + _dot_tn(bm, vals * jnp.exp(b_last - b_w))

        y = (y + dskip_ref[...] * xs) * _silu(z_ref[rows, :])
        yn = y * lax.rsqrt(jnp.mean(y * y, axis=-1, keepdims=True) + EPS) * nw_ref[...]
        y_ref[rows, :] = jnp.where(valid_col, yn, 0.0).astype(y_ref.dtype)
        return s_new

    if has_init:
        for i in range(nseq):
            s = jnp.concatenate([s0_ref[i, r] for r in range(rep)], axis=1)
            s_new = one_seq(i, s)
            for r in range(rep):
                s_out_ref[i, r] = s_new[:, r * headdim:(r + 1) * headdim]
    else:
        @pl.when(c == 0)
        def _():
            s_ref[...] = jnp.zeros(s_ref.shape, F32)

        s_new = one_seq(0, s_ref[...])
        s_ref[...] = s_new

        @pl.when(c == pl.num_programs(2) - 1)
        def _():
            s_out_ref[0, 0] = s_new


def _ssd_scan(act, proj, dtb, alog, dskip, norm_w, s0, ybuf, *, nb, nc, chunk, pad, row_blk0, nseq=1):
    dm = _dims()
    inner, nheads, rep = dm["ssd_inner"], dm["ssd_heads"], dm["ssd_rep"]
    assert rep == SUBLANES and nheads <= LANES and SSD_STATE == LANES
    assert nseq == 1 or (nc == 1 and s0 is not None)
    gw = rep * SSD_HEADDIM
    ng = SSD_GROUPS
    nt = proj.shape[0]
    bm_blk0 = inner // SSD_STATE
    dt_blk = (inner + dm["ssd_conv"]) // LANES
    rb = nseq * chunk

    def loc(b, c):
        return b * nc + c

    in_specs = [pl.BlockSpec((rb, gw), lambda b, g, c: (loc(b, c), g)),
                pl.BlockSpec((rb, SSD_STATE), lambda b, g, c: (loc(b, c), bm_blk0 + g)),
                pl.BlockSpec((rb, SSD_STATE), lambda b, g, c: (loc(b, c), bm_blk0 + ng + g)),
                pl.BlockSpec((rb, gw), lambda b, g, c: (row_blk0 + loc(b, c), g)),
                pl.BlockSpec((rb, LANES), lambda b, g, c: (row_blk0 + loc(b, c), dt_blk)),
                pl.BlockSpec((1, LANES), lambda b, g, c: (0, 0)),
                pl.BlockSpec((1, LANES), lambda b, g, c: (0, 0)),
                pl.BlockSpec((1, gw), lambda b, g, c: (0, g)),
                pl.BlockSpec((1, gw), lambda b, g, c: (0, g))]
    args = [act, act, act, proj, proj, dtb, alog, dskip, norm_w]
    aliases = {}
    if s0 is not None:
        state_spec = pl.BlockSpec((nseq, rep, SSD_STATE, SSD_HEADDIM), lambda b, g, c: (b, g, 0, 0))
        state_shape = jax.ShapeDtypeStruct((nb * nseq, nheads, SSD_STATE, SSD_HEADDIM), F32)
        in_specs.append(state_spec)
        args.append(s0)
    else:
        state_spec = pl.BlockSpec((1, 1, SSD_STATE, gw), lambda b, g, c: (b, g, 0, 0))
        state_shape = jax.ShapeDtypeStruct((nb, ng, SSD_STATE, gw), F32)
    if ybuf is not None:
        in_specs.append(pl.BlockSpec(memory_space=pl.ANY))
        args.append(ybuf)
        aliases = {len(args) - 1: 0}
    return pl.pallas_call(
        functools.partial(_ssd_scan_kernel, chunk=chunk, pad=pad, has_init=s0 is not None,
                          has_ybuf=ybuf is not None, n_heads=nheads, rep=rep, headdim=SSD_HEADDIM),
        grid=(nb, ng, nc),
        in_specs=in_specs,
        out_specs=[pl.BlockSpec((rb, gw), lambda b, g, c: (row_blk0 + loc(b, c), g)), state_spec],
        out_shape=[jax.ShapeDtypeStruct((nt, inner), F32), state_shape],
        scratch_shapes=[pltpu.VMEM((SSD_STATE, gw), F32)],
        input_output_aliases=aliases,
        compiler_params=_cparams(("parallel", "parallel", "arbitrary")),
        name="ssd_scan",
    )(*args)


def _ssd_state_from_wide(s, nb):
    rep = s.shape[3] // SSD_HEADDIM
    s = s.reshape(nb, SSD_GROUPS, SSD_STATE, rep, SSD_HEADDIM)
    return jnp.transpose(s, (0, 1, 3, 2, 4)).reshape(nb, SSD_GROUPS * rep, SSD_STATE, SSD_HEADDIM)


def _ssd_mixer(hn, x, st, w_in, conv_w, conv_b, dt_bias, a_log, d_skip, norm_w, w_out):
    dm = _dims()
    inner, cdim, nheads, ntp = dm["ssd_inner"], dm["ssd_conv"], dm["ssd_heads"], dm["ntp"]
    s0, conv0 = st
    proj = _mm(hn, w_in[None], 0)
    conv_p, conv_s = _conv_tails(proj, inner, cdim)
    cb = conv_b.reshape(1, cdim)
    cache_rows = _sample_rows(conv0, SAMPLE_PAD - (CONV_W - 1))
    padl = lambda a: jnp.pad(a, (0, LANES - nheads)).reshape(1, LANES)
    dtb, alog = padl(dt_bias), padl(a_log)
    dskip = jnp.repeat(d_skip, SSD_HEADDIM).reshape(1, inner)
    nw = norm_w.reshape(1, inner)

    chunk = _tile(SEQ, PROMPT_CHUNK, SUBLANES)
    act_p = _conv_silu(proj, inner, None, conv_w, cb, sample=False)
    y, s_p = _ssd_scan(act_p, proj, dtb, alog, dskip, nw, None, None,
                       nb=BATCH, nc=SEQ // chunk, chunk=chunk, pad=0, row_blk0=0)
    act_s = _conv_silu(proj, inner, cache_rows, conv_w, cb, sample=True)
    nseq = _tile(DEC_BATCH, SAMPLE_SEQS_PER_STEP, 1)
    y, s_s = _ssd_scan(act_s, proj, dtb, alog, dskip, nw, s0, y,
                       nb=DEC_BATCH // nseq, nc=1, chunk=SAMPLE_ROWS, pad=SAMPLE_PAD,
                       row_blk0=ntp // (nseq * SAMPLE_ROWS), nseq=nseq)
    x = _mm(y, w_out[None], 0, res=x, scale=1.0, tm_pref=256)
    return x, (_ssd_state_from_wide(s_p, BATCH), conv_p), (s_s, conv_s)


def kernel(x_prompt, x_sample, state_ret, state_mlstm_c, state_mlstm_n, state_mlstm_m, cache_mlstm_conv,
           state_rglru_h, cache_rglru_conv, state_ssd, cache_ssd_conv,
           norm_ffn1, w_ffn1_up, w_ffn1_down, norm_mix, norm_ffn2, w_ffn2_up, w_ffn2_down, norm_final,
           ret_w_in, ret_norm, ret_w_out,
           ml_w_up, ml_conv_w, ml_conv_b, ml_w_q, ml_w_k, ml_w_v, ml_w_gates, ml_b_gates, ml_skip, ml_norm, ml_w_down,
           rg_w_in, rg_conv_w, rg_conv_b, rg_w_a, rg_b_a, rg_w_x, rg_b_x, rg_lambda, rg_w_out,
           ssd_w_in, ssd_conv_w, ssd_conv_b, ssd_dt_bias, ssd_a_log, ssd_d, ssd_norm, ssd_w_out):
    d = D_MODEL
    ntp = BATCH * SEQ
    xs = jnp.pad(x_sample, ((0, 0), (SAMPLE_PAD, 0), (0, 0))).reshape(DEC_BATCH * SAMPLE_ROWS, d)
    x = jnp.concatenate([x_prompt.reshape(ntp, d), xs], axis=0)

    n1 = norm_ffn1.reshape(DEPTH, 1, d)
    nm = norm_mix.reshape(DEPTH, 1, d)
    n2 = norm_ffn2.reshape(DEPTH, 1, d)
    states_p = {}
    states_s = {}
    for i in range(DEPTH):
        x = _ffn(x, n1, w_ffn1_up, w_ffn1_down, i)
        hn = _rmsnorm(x, nm, i, MXU_DTYPE)
        kind = i % N_MIXERS
        if kind == 0:
            x, states_p["ret"], states_s["ret"] = _retention_mixer(hn, x, state_ret, ret_w_in, ret_norm, ret_w_out)
        elif kind == 1:
            x, states_p["ml"], states_s["ml"] = _mlstm_mixer(
                hn, x, (state_mlstm_c, state_mlstm_n, state_mlstm_m, cache_mlstm_conv),
                ml_w_up, ml_conv_w, ml_conv_b, ml_w_q, ml_w_k, ml_w_v, ml_w_gates, ml_b_gates, ml_skip, ml_norm,
                ml_w_down)
        elif kind == 2:
            x, states_p["rg"], states_s["rg"] = _rglru_mixer(
                hn, x, (state_rglru_h, cache_rglru_conv), rg_w_in, rg_conv_w, rg_conv_b, rg_w_a, rg_b_a,
                rg_w_x, rg_b_x, rg_lambda, rg_w_out)
        else:
            x, states_p["ssd"], states_s["ssd"] = _ssd_mixer(
                hn, x, (state_ssd, cache_ssd_conv), ssd_w_in, ssd_conv_w, ssd_conv_b, ssd_dt_bias, ssd_a_log,
                ssd_d, ssd_norm, ssd_w_out)
        x = _ffn(x, n2, w_ffn2_up, w_ffn2_down, i)
    y = _rmsnorm(x, norm_final.reshape(1, 1, d), 0, F32)
    y_prompt = y[:ntp].reshape(BATCH, SEQ, d)
    y_sample = y[ntp:].reshape(DEC_BATCH, SAMPLE_ROWS, d)[:, SAMPLE_PAD:]

    def flat(st):
        return (st["ret"],) + tuple(st["ml"]) + tuple(st["rg"]) + tuple(st["ssd"])

    return (y_prompt, y_sample) + flat(states_p) + flat(states_s)
```

```python
import functools

import jax
import jax.numpy as jnp
from jax import lax
from jax.experimental import pallas as pl
from jax.experimental.pallas import tpu as pltpu

D_MODEL = 2048
BATCH = 4
SEQ = 2048
DEPTH = 4
DEC_BATCH = 128
DEC_SEQ = 4
PAST_LEN = 16384
N_MIXERS = 4
D_FF = 5632
EPS = 1e-6
CONV_W = 4

RET_HEADS = 8
ROPE_BASE = 10000.0
ML_HEADS = 4
ML_QKV_BLOCK = 4
RG_HEADS = 10
RG_C = 8.0
SSD_HEADDIM = 64
SSD_GROUPS = 8
SSD_STATE = 128

F32 = jnp.float32
MXU_DTYPE = jnp.bfloat16
HIGHEST = lax.Precision.HIGHEST

SUBLANES = 8
LANES = 128
MXU_DIM = 256
VMEM_LIMIT = 56 * 1024 * 1024

SAMPLE_ROWS = SUBLANES
SAMPLE_PAD = SAMPLE_ROWS - DEC_SEQ
HALO = SUBLANES

ROW_TILE = 1088
NORM_ROW_TILE = 512
COL_TILE = 512
WIDE_COL_TILE = 1024
SAMPLE_SEQS_PER_STEP = 4
PROMPT_CHUNK = 256
PRE_ROW_TILE = 128


def _dims():
    d = D_MODEL
    ret_dk = d // RET_HEADS
    ml_inner = 2 * d
    rg_width = d * 5 // 4
    ssd_inner = 2 * d
    ssd_heads = ssd_inner // SSD_HEADDIM
    return dict(
        ret_dk=ret_dk, ret_dv=2 * ret_dk,
        ml_inner=ml_inner, ml_dh=ml_inner // ML_HEADS,
        rg_width=rg_width, rg_block=rg_width // RG_HEADS,
        ssd_inner=ssd_inner, ssd_heads=ssd_heads, ssd_rep=ssd_heads // SSD_GROUPS,
        ssd_conv=ssd_inner + 2 * SSD_GROUPS * SSD_STATE,
        ntp=BATCH * SEQ, nts=DEC_BATCH * SAMPLE_ROWS,
    )


def _tile(n, pref, mult):
    t = min(pref, n) // mult * mult
    while t > mult and n % t:
        t -= mult
    assert t >= mult and n % t == 0, (n, pref, mult)
    return t


def _cparams(sem):
    return pltpu.CompilerParams(dimension_semantics=sem, vmem_limit_bytes=VMEM_LIMIT)


def _silu(x):
    return x * jax.nn.sigmoid(x)


def _softplus(x):
    return jnp.maximum(x, 0.0) + jnp.log1p(jnp.exp(-jnp.abs(x)))


def _dot(a, b):
    return jnp.dot(a.astype(MXU_DTYPE), b.astype(MXU_DTYPE), preferred_element_type=F32)


def _dot_nt(a, b):
    return lax.dot_general(a.astype(MXU_DTYPE), b.astype(MXU_DTYPE), (((1,), (1,)), ((), ())),
                           preferred_element_type=F32)


def _dot_tn(a, b):
    return lax.dot_general(a.astype(MXU_DTYPE), b.astype(MXU_DTYPE), (((0,), (0,)), ((), ())),
                           preferred_element_type=F32)


def _dot_exact(a, b):
    return jnp.dot(a, b, preferred_element_type=F32, precision=HIGHEST)


def _dot_nt_exact(a, b):
    return lax.dot_general(a, b, (((1,), (1,)), ((), ())), preferred_element_type=F32, precision=HIGHEST)


def _cumsum_rows(x):
    n = x.shape[0]
    row = lax.broadcasted_iota(jnp.int32, x.shape, 0)
    sh = 1
    while sh < n:
        x = x + jnp.where(row >= sh, pltpu.roll(x, sh, 0), 0.0)
        sh *= 2
    return x


def _linear_scan_rows(a, b, span, pos):
    sh = 1
    while sh < span:
        ok = pos >= sh
        a_prev = jnp.where(ok, pltpu.roll(a, sh, 0), 1.0)
        b_prev = jnp.where(ok, pltpu.roll(b, sh, 0), 0.0)
        b = b + a * b_prev
        a = a * a_prev
        sh *= 2
    return a, b


def _causal_conv(x, halo, win_ref, w_ref, b_ref):
    tm = x.shape[0]
    win_ref[0:HALO, :] = halo
    win_ref[HALO:HALO + tm, :] = x
    y = b_ref[...] + x * w_ref[CONV_W - 1:CONV_W, :]
    for j in range(1, CONV_W):
        y = y + win_ref[pl.ds(HALO - j, tm), :] * w_ref[CONV_W - 1 - j:CONV_W - j, :]
    return y


def _token_rows(n):
    row = lax.broadcasted_iota(jnp.int32, (n, 1), 0)
    return (row % SAMPLE_ROWS) >= SAMPLE_PAD


def _rmsnorm_kernel(x_ref, w_ref, o_ref):
    x = x_ref[...]
    ms = jnp.mean(x * x, axis=-1, keepdims=True)
    o_ref[...] = (x * lax.rsqrt(ms + EPS) * w_ref[...]).astype(o_ref.dtype)


def _rmsnorm(x, w, layer, out_dtype):
    m, d = x.shape
    tm = _tile(m, NORM_ROW_TILE, 16)
    return pl.pallas_call(
        _rmsnorm_kernel,
        grid=(m // tm,),
        in_specs=[pl.BlockSpec((tm, d), lambda i: (i, 0)),
                  pl.BlockSpec((None, 1, d), lambda i: (layer, 0, 0))],
        out_specs=pl.BlockSpec((tm, d), lambda i: (i, 0)),
        out_shape=jax.ShapeDtypeStruct((m, d), out_dtype),
        compiler_params=_cparams(("parallel",)),
        name="rmsnorm",
    )(x, w)


def _mm_kernel(a_ref, w_ref, *rest, scale, has_res):
    if has_res:
        r_ref, o_ref, wb_ref = rest
    else:
        o_ref, wb_ref = rest

    @pl.when(pl.program_id(1) == 0)
    def _():
        wb_ref[...] = w_ref[...].astype(MXU_DTYPE)

    acc = jnp.dot(a_ref[...].astype(MXU_DTYPE), wb_ref[...], preferred_element_type=F32)
    if has_res:
        acc = r_ref[...] + scale * acc
    o_ref[...] = acc.astype(o_ref.dtype)


def _mm(a, w, layer, *, res=None, scale=1.0, tm_pref=ROW_TILE, tn_pref=WIDE_COL_TILE):
    m, k = a.shape
    n = w.shape[2]
    tm = _tile(m, tm_pref, 16)
    tn = min(tn_pref, n)
    nj = pl.cdiv(n, tn)
    w_mode = dict(pipeline_mode=pl.Buffered(1)) if res is not None else {}
    in_specs = [pl.BlockSpec((tm, k), lambda j, i: (i, 0)),
                pl.BlockSpec((None, k, tn), lambda j, i: (layer, 0, j), **w_mode)]
    args = [a, w]
    aliases = {}
    if res is not None:
        assert n % tn == 0 and res.shape == (m, n)
        in_specs.append(pl.BlockSpec((tm, tn), lambda j, i: (i, j)))
        args.append(res)
        aliases = {2: 0}
    return pl.pallas_call(
        functools.partial(_mm_kernel, scale=scale, has_res=res is not None),
        grid=(nj, m // tm),
        in_specs=in_specs,
        out_specs=pl.BlockSpec((tm, tn), lambda j, i: (i, j)),
        out_shape=jax.ShapeDtypeStruct((m, nj * tn), F32),
        scratch_shapes=[pltpu.VMEM((k, tn), MXU_DTYPE)],
        input_output_aliases=aliases,
        compiler_params=_cparams(("parallel", "arbitrary")),
        name="matmul",
    )(*args)


def _swiglu_up_kernel(a_ref, wg_ref, wu_ref, o_ref, wgb_ref, wub_ref):
    @pl.when(pl.program_id(1) == 0)
    def _():
        wgb_ref[...] = wg_ref[...].astype(MXU_DTYPE)
        wub_ref[...] = wu_ref[...].astype(MXU_DTYPE)

    a = a_ref[...]
    gate = jnp.dot(a, wgb_ref[...], preferred_element_type=F32)
    up = jnp.dot(a, wub_ref[...], preferred_element_type=F32)
    o_ref[...] = (_silu(gate) * up).astype(o_ref.dtype)


def _swiglu_up(a, w_up, layer):
    m, k = a.shape
    ff = w_up.shape[2] // 2
    tm = _tile(m, ROW_TILE, 16)
    tn = _tile(ff, COL_TILE, LANES)
    nj = ff // tn
    return pl.pallas_call(
        _swiglu_up_kernel,
        grid=(nj, m // tm),
        in_specs=[pl.BlockSpec((tm, k), lambda j, i: (i, 0)),
                  pl.BlockSpec((None, k, tn), lambda j, i: (layer, 0, j)),
                  pl.BlockSpec((None, k, tn), lambda j, i: (layer, 0, nj + j))],
        out_specs=pl.BlockSpec((tm, tn), lambda j, i: (i, j)),
        out_shape=jax.ShapeDtypeStruct((m, ff), MXU_DTYPE),
        scratch_shapes=[pltpu.VMEM((k, tn), MXU_DTYPE), pltpu.VMEM((k, tn), MXU_DTYPE)],
        compiler_params=_cparams(("parallel", "arbitrary")),
        name="swiglu_up",
    )(a, w_up, w_up)


def _ffn(x, norm_w, w_up, w_down, layer):
    h = _rmsnorm(x, norm_w, layer, MXU_DTYPE)
    act = _swiglu_up(h, w_up, layer)
    return _mm(act, w_down, layer, res=x, scale=0.5, tm_pref=256)


def _ret_scan_kernel(lg_ref, q_ref, k_ref, v_ref, g_ref, cos_ref, sin_ref, nw_ref, *rest,
                     chunk, pad, has_init, has_ybuf, k_scale):
    rest = list(rest)
    s0_ref = rest.pop(0) if has_init else None
    if has_ybuf:
        rest.pop(0)
    y_ref, s_out_ref, s_ref = rest
    h = pl.program_id(1)
    c = pl.program_id(2)
    nvalid = chunk - pad
    nseq = s_out_ref.shape[0]
    lam = lg_ref[h]
    half = cos_ref.shape[1]

    t_col = lax.broadcasted_iota(jnp.int32, (chunk, 1), 0)
    cnt_col = jnp.maximum(t_col - pad + 1, 0).astype(F32)
    t_mat = lax.broadcasted_iota(jnp.int32, (chunk, chunk), 0)
    s_mat = lax.broadcasted_iota(jnp.int32, (chunk, chunk), 1)
    cnt_t = jnp.maximum(t_mat - pad + 1, 0).astype(F32)
    cnt_s = jnp.maximum(s_mat - pad + 1, 0).astype(F32)
    w = jnp.where((t_mat >= s_mat) & (s_mat >= pad), jnp.exp(lam * (cnt_t - cnt_s)), 0.0)
    ws = jnp.where(t_col >= pad, jnp.exp(lam * (nvalid - cnt_col)), 0.0)
    w_in = jnp.exp(lam * cnt_col)
    w_state = jnp.exp(lam * nvalid)

    def one_seq(i, s):
        r = slice(i * chunk, (i + 1) * chunk)
        cos = cos_ref[r, :]
        sin = sin_ref[r, :]

        def rot(x):
            x1 = x[:, :half]
            x2 = x[:, half:]
            return jnp.concatenate([x1 * cos - x2 * sin, x1 * sin + x2 * cos], axis=-1)

        q = rot(q_ref[r, :])
        k = rot(k_ref[r, :]) * k_scale
        v = v_ref[r, :]
        y = _dot(_dot_nt(q, k) * w, v) + w_in * _dot(q, s)
        s_new = w_state * s + _dot_tn(k * ws, v)
        mu = jnp.mean(y, axis=-1, keepdims=True)
        yc = y - mu
        yn = yc * lax.rsqrt(jnp.mean(yc * yc, axis=-1, keepdims=True) + EPS) * nw_ref[...]
        out = _silu(g_ref[r, :]) * yn
        y_ref[r, :] = jnp.where(t_col >= pad, out, 0.0).astype(y_ref.dtype)
        return s_new

    if has_init:
        for i in range(nseq):
            s_out_ref[i, 0] = one_seq(i, s0_ref[i, 0])
    else:
        @pl.when(c == 0)
        def _():
            s_ref[...] = jnp.zeros(s_ref.shape, F32)

        s_new = one_seq(0, s_ref[...])
        s_ref[...] = s_new

        @pl.when(c == pl.num_programs(2) - 1)
        def _():
            s_out_ref[0, 0] = s_new


def _ret_scan(proj, cos, sin, log_gamma, norm_w, s0, ybuf, *, nb, nc, chunk, pad, row_blk0, nseq=1):
    dm = _dims()
    dk, dv, nh = dm["ret_dk"], dm["ret_dv"], RET_HEADS
    nt = proj.shape[0]
    assert nseq == 1 or (nc == 1 and s0 is not None)
    rb = nseq * chunk

    def rows(b, c):
        return row_blk0 + b * nc + c

    in_specs = [pl.BlockSpec(memory_space=pltpu.SMEM),
                pl.BlockSpec((rb, dk), lambda b, h, c: (rows(b, c), h)),
                pl.BlockSpec((rb, dk), lambda b, h, c: (rows(b, c), nh + h)),
                pl.BlockSpec((rb, dv), lambda b, h, c: (rows(b, c), nh + h)),
                pl.BlockSpec((rb, dv), lambda b, h, c: (rows(b, c), 2 * nh + h)),
                pl.BlockSpec((rb, dk // 2), lambda b, h, c: (rows(b, c), 0)),
                pl.BlockSpec((rb, dk // 2), lambda b, h, c: (rows(b, c), 0)),
                pl.BlockSpec((1, dv), lambda b, h, c: (0, h))]
    args = [log_gamma, proj, proj, proj, proj, cos, sin, norm_w]
    aliases = {}
    if s0 is not None:
        in_specs.append(pl.BlockSpec((nseq, 1, dk, dv), lambda b, h, c: (b, h, 0, 0)))
        args.append(s0)
    if ybuf is not None:
        in_specs.append(pl.BlockSpec(memory_space=pl.ANY))
        args.append(ybuf)
        aliases = {len(args) - 1: 0}
    return pl.pallas_call(
        functools.partial(_ret_scan_kernel, chunk=chunk, pad=pad, has_init=s0 is not None,
                          has_ybuf=ybuf is not None, k_scale=dk ** -0.5),
        grid=(nb, nh, nc),
        in_specs=in_specs,
        out_specs=[pl.BlockSpec((rb, dv), lambda b, h, c: (rows(b, c), h)),
                   pl.BlockSpec((nseq, 1, dk, dv), lambda b, h, c: (b, h, 0, 0))],
        out_shape=[jax.ShapeDtypeStruct((nt, nh * dv), F32),
                   jax.ShapeDtypeStruct((nb * nseq, nh, dk, dv), F32)],
        scratch_shapes=[pltpu.VMEM((dk, dv), F32)],
        input_output_aliases=aliases,
        compiler_params=_cparams(("parallel", "parallel", "arbitrary")),
        name="retention_scan",
    )(*args)


def _pad_sample(a):
    ntp = BATCH * SEQ
    s = lax.slice(a, (ntp, 0), (ntp + DEC_BATCH * DEC_SEQ, a.shape[1]))
    s = jnp.pad(s.reshape(DEC_BATCH, DEC_SEQ, a.shape[1]), ((0, 0), (SAMPLE_PAD, 0), (0, 0)))
    return s.reshape(DEC_BATCH * SAMPLE_ROWS, a.shape[1])


def _merge_sample(y, y_s):
    ntp = BATCH * SEQ
    tok = y_s.reshape(DEC_BATCH, SAMPLE_ROWS, y_s.shape[1])[:, SAMPLE_PAD:]
    return lax.dynamic_update_slice(y, tok.reshape(DEC_BATCH * DEC_SEQ, y_s.shape[1]), (ntp, 0))


def _retention_mixer(hn, x, state_ret, w_in, norm_w, w_out):
    dm = _dims()
    dk = dm["ret_dk"]
    proj = _mm(hn, w_in[None], 0)
    proj_s = _pad_sample(proj)

    half = dk // 2
    inv = ROPE_BASE ** (-jnp.arange(half, dtype=F32) / half)

    def tables(pos):
        ang = pos.astype(F32)[:, None] * inv
        return jnp.cos(ang), jnp.sin(ang)

    cos_p, sin_p = tables(jnp.tile(jnp.arange(SEQ), BATCH))
    cos_s, sin_s = tables(jnp.tile(PAST_LEN - SAMPLE_PAD + jnp.arange(SAMPLE_ROWS), DEC_BATCH))
    log_gamma = jnp.log1p(-(2.0 ** (-5.0 - jnp.arange(RET_HEADS, dtype=F32))))
    nw = norm_w.reshape(1, -1)

    chunk = _tile(SEQ, PROMPT_CHUNK, SUBLANES)
    y, s_p = _ret_scan(proj, cos_p, sin_p, log_gamma, nw, None, None,
                       nb=BATCH, nc=SEQ // chunk, chunk=chunk, pad=0, row_blk0=0)
    nseq = _tile(DEC_BATCH, SAMPLE_SEQS_PER_STEP, 1)
    y_s, s_s = _ret_scan(proj_s, cos_s, sin_s, log_gamma, nw, state_ret, None,
                         nb=DEC_BATCH // nseq, nc=1, chunk=SAMPLE_ROWS, pad=SAMPLE_PAD, row_blk0=0, nseq=nseq)
    x = _mm(_merge_sample(y, y_s), w_out[None], 0, res=x, scale=1.0, tm_pref=256)
    return x, s_p, s_s


def _ml_pre_kernel(xm_ref, halo_ref, *rest, sample, seq_len, k_scale, nblk, bw):
    rest = list(rest)
    cache_ref = rest.pop(0) if sample else None
    (cw_ref, cb_ref, wq_ref, wk_ref, wv_ref, wg_ref, bg_ref,
     q_ref, k_ref, v_ref, xc_ref, gates_ref, win_ref) = rest
    tm = xm_ref.shape[0]
    xm = xm_ref[...]
    if sample:
        xm = jnp.where(_token_rows(tm), xm, cache_ref[...])
        halo = jnp.zeros(halo_ref.shape, F32)
    else:
        seq_start = (pl.program_id(0) * tm) % seq_len == 0
        halo = jnp.where(seq_start, 0.0, halo_ref[...])
    xc = _silu(_causal_conv(xm, halo, win_ref, cw_ref, cb_ref))
    xc_ref[...] = xc

    for j in range(nblk):
        sl = slice(j * bw, (j + 1) * bw)
        q_ref[:, sl] = _dot(xc[:, sl], wq_ref[j])
        k_ref[:, sl] = _dot(xc[:, sl], wk_ref[j])
        v_ref[:, sl] = _dot(xm[:, sl], wv_ref[j])
    inner = q_ref.shape[1]
    g = (_dot(q_ref[...], wg_ref[0:inner, :]) + _dot(k_ref[...], wg_ref[inner:2 * inner, :])
         + _dot(v_ref[...], wg_ref[2 * inner:3 * inner, :]) + bg_ref[...])
    k_ref[...] = k_ref[...] * k_scale
    lane = lax.broadcasted_iota(jnp.int32, g.shape, 1)
    log_f = jnp.minimum(g, 0.0) - jnp.log1p(jnp.exp(-jnp.abs(g)))
    gates_ref[...] = jnp.where(lane < ML_HEADS, g, jnp.where(lane < 2 * ML_HEADS, log_f, 0.0))


def _ml_pre(up, cache_rows, cw, cb, wq, wk, wv, wg, bg, *, sample):
    dm = _dims()
    inner, dh = dm["ml_inner"], dm["ml_dh"]
    n = dm["nts"] if sample else dm["ntp"]
    tm = _tile(n if sample else SEQ, PRE_ROW_TILE, SAMPLE_ROWS)
    hb = tm // HALO
    nblk, bw = wq.shape[0], wq.shape[1]

    in_specs = [pl.BlockSpec((tm, inner), lambda i: (i, 0)),
                pl.BlockSpec((HALO, inner), lambda i: (jnp.maximum(i * hb - 1, 0), 0))]
    args = [up, up]
    if sample:
        in_specs.append(pl.BlockSpec((tm, inner), lambda i: (i, 0)))
        args.append(cache_rows)
    const2 = lambda i: (0, 0)
    const3 = lambda i: (0, 0, 0)
    in_specs += [pl.BlockSpec(cw.shape, const2), pl.BlockSpec(cb.shape, const2),
                 pl.BlockSpec(wq.shape, const3), pl.BlockSpec(wk.shape, const3), pl.BlockSpec(wv.shape, const3),
                 pl.BlockSpec(wg.shape, const2), pl.BlockSpec(bg.shape, const2)]
    args += [cw, cb, wq, wk, wv, wg, bg]
    row_spec = pl.BlockSpec((tm, inner), lambda i: (i, 0))
    return pl.pallas_call(
        functools.partial(_ml_pre_kernel, sample=sample, seq_len=SEQ, k_scale=dh ** -0.5, nblk=nblk, bw=bw),
        grid=(n // tm,),
        in_specs=in_specs,
        out_specs=[row_spec, row_spec, row_spec, row_spec, pl.BlockSpec((tm, LANES), lambda i: (i, 0))],
        out_shape=[jax.ShapeDtypeStruct((n, inner), F32)] * 4 + [jax.ShapeDtypeStruct((n, LANES), F32)],
        scratch_shapes=[pltpu.VMEM((HALO + tm, inner), F32)],
        compiler_params=_cparams(("arbitrary",)),
        name="mlstm_pre",
    )(*args)


def _ml_scan_kernel(m0_ref, q_ref, k_ref, v_ref, xc_ref, z_ref, gates_ref, skip_ref, nw_ref, *rest,
                    chunk, pad, has_init, has_ybuf):
    rest = list(rest)
    if has_init:
        c0_ref, n0_ref = rest.pop(0), rest.pop(0)
    if has_ybuf:
        rest.pop(0)
    y_ref, c_out_ref, n_out_ref, m_out_ref, c_ref, n_ref, m_ref = rest
    b = pl.program_id(0)
    h = pl.program_id(1)
    c = pl.program_id(2)

    @pl.when(c == 0)
    def _():
        if has_init:
            c_ref[...] = c0_ref[0, 0]
            n_ref[...] = n0_ref[0, 0]
            m_ref[...] = jnp.full(m_ref.shape, m0_ref[b * ML_HEADS + h], F32)
        else:
            c_ref[...] = jnp.zeros(c_ref.shape, F32)
            n_ref[...] = jnp.zeros(n_ref.shape, F32)
            m_ref[...] = jnp.zeros(m_ref.shape, F32)

    q = q_ref[...]
    k = k_ref[...]
    v = v_ref[...]
    t_col = lax.broadcasted_iota(jnp.int32, (chunk, 1), 0)
    valid_col = t_col >= pad

    gates = jnp.where(valid_col, gates_ref[...], 0.0)
    csum = _cumsum_rows(gates)
    lane = lax.broadcasted_iota(jnp.int32, gates.shape, 1)
    ig_col = jnp.sum(jnp.where(lane == h, gates, 0.0), axis=1, keepdims=True)
    b_col = jnp.sum(jnp.where(lane == ML_HEADS + h, csum, 0.0), axis=1, keepdims=True)
    ig_col = jnp.where(valid_col, ig_col, -jnp.inf)
    sel = (lax.broadcasted_iota(jnp.int32, (SUBLANES, LANES), 0)
           == lax.broadcasted_iota(jnp.int32, (SUBLANES, LANES), 1)).astype(F32)
    sub = lax.broadcasted_iota(jnp.int32, (SUBLANES, chunk), 0)
    ig_row = jnp.sum(jnp.where(sub == h, _dot_nt_exact(sel, gates), 0.0), axis=0, keepdims=True)
    b_row = jnp.sum(jnp.where(sub == ML_HEADS + h, _dot_nt_exact(sel, csum), 0.0), axis=0, keepdims=True)
    s_row = lax.broadcasted_iota(jnp.int32, (1, chunk), 1)
    ig_row = jnp.where(s_row >= pad, ig_row, -jnp.inf)

    t_mat = lax.broadcasted_iota(jnp.int32, (chunk, chunk), 0)
    s_mat = lax.broadcasted_iota(jnp.int32, (chunk, chunk), 1)
    d = jnp.where(t_mat >= s_mat, b_col - b_row + ig_row, -jnp.inf)
    m_prev = m_ref[...][:, 0:1]
    inter = b_col + m_prev
    m_t = jnp.maximum(inter, jnp.max(d, axis=1, keepdims=True))
    sc = _dot_nt(q, k) * jnp.exp(d - m_t)
    w_inter = jnp.exp(inter - m_t)
    cmat = c_ref[...]
    nvec = n_ref[...]
    num = _dot(sc, v) + w_inter * _dot(q, cmat)
    den = jnp.sum(sc, axis=1, keepdims=True) + w_inter * jnp.sum(q * nvec, axis=1, keepdims=True)
    hid = num / jnp.maximum(jnp.abs(den), jnp.exp(-m_t))

    m_new = m_t[chunk - 1:chunk, :]
    b_last = b_col[chunk - 1:chunk, :]
    ws = jnp.exp(b_last - b_col + ig_col - m_new)
    decay = jnp.exp(b_last + m_prev - m_new)
    kw = k * ws
    c_new = decay * cmat + _dot_tn(kw, v)
    n_new = decay * nvec + jnp.sum(kw, axis=0, keepdims=True)
    c_ref[...] = c_new
    n_ref[...] = n_new
    m_ref[...] = jnp.broadcast_to(m_new, m_ref.shape)

    mu = jnp.mean(hid, axis=-1, keepdims=True)
    hc = hid - mu
    hn = hc * lax.rsqrt(jnp.mean(hc * hc, axis=-1, keepdims=True) + EPS) * nw_ref[...]
    out = (hn + skip_ref[...] * xc_ref[...]) * _silu(z_ref[...])
    y_ref[...] = jnp.where(valid_col, out, 0.0).astype(y_ref.dtype)

    @pl.when(c == pl.num_programs(2) - 1)
    def _():
        c_out_ref[0, 0] = c_new
        n_out_ref[0, 0] = n_new
        m_out_ref[0, 0] = jnp.broadcast_to(m_new, (1, LANES))


def _ml_scan(q, k, v, xc, up, gates, skip, norm_w, init, ybuf, *, nb, nc, chunk, pad, row_blk0):
    dm = _dims()
    dh, nh, inner = dm["ml_dh"], ML_HEADS, dm["ml_inner"]
    nt = up.shape[0]

    def loc(b, h, c):
        return (b * nc + c, h)

    head_spec = pl.BlockSpec((chunk, dh), loc)
    in_specs = [pl.BlockSpec(memory_space=pltpu.SMEM),
                head_spec, head_spec, head_spec, head_spec,
                pl.BlockSpec((chunk, dh), lambda b, h, c: (row_blk0 + b * nc + c, nh + h)),
                pl.BlockSpec((chunk, LANES), lambda b, h, c: (b * nc + c, 0)),
                pl.BlockSpec((1, dh), lambda b, h, c: (0, h)),
                pl.BlockSpec((1, dh), lambda b, h, c: (0, h))]
    has_init = init is not None
    if has_init:
        c0, n0, m0 = init
    else:
        c0 = n0 = None
        m0 = jnp.zeros((1,), F32)
    args = [m0, q, k, v, xc, up, gates, skip, norm_w]
    if has_init:
        in_specs += [pl.BlockSpec((1, 1, dh, dh), lambda b, h, c: (b, h, 0, 0)),
                     pl.BlockSpec((1, 1, 1, dh), lambda b, h, c: (b, h, 0, 0))]
        args += [c0, n0]
    aliases = {}
    if ybuf is not None:
        in_specs.append(pl.BlockSpec(memory_space=pl.ANY))
        args.append(ybuf)
        aliases = {len(args) - 1: 0}
    return pl.pallas_call(
        functools.partial(_ml_scan_kernel, chunk=chunk, pad=pad, has_init=has_init, has_ybuf=ybuf is not None),
        grid=(nb, nh, nc),
        in_specs=in_specs,
        out_specs=[pl.BlockSpec((chunk, dh), lambda b, h, c: (row_blk0 + b * nc + c, h)),
                   pl.BlockSpec((1, 1, dh, dh), lambda b, h, c: (b, h, 0, 0)),
                   pl.BlockSpec((1, 1, 1, dh), lambda b, h, c: (b, h, 0, 0)),
                   pl.BlockSpec((1, 1, 1, LANES), lambda b, h, c: (b, h, 0, 0))],
        out_shape=[jax.ShapeDtypeStruct((nt, inner), F32),
                   jax.ShapeDtypeStruct((nb, nh, dh, dh), F32),
                   jax.ShapeDtypeStruct((nb, nh, 1, dh), F32),
                   jax.ShapeDtypeStruct((nb, nh, 1, LANES), F32)],
        scratch_shapes=[pltpu.VMEM((dh, dh), F32), pltpu.VMEM((1, dh), F32), pltpu.VMEM((1, LANES), F32)],
        input_output_aliases=aliases,
        compiler_params=_cparams(("parallel", "parallel", "arbitrary")),
        name="mlstm_scan",
    )(*args)


def _expand_block_diag(w, bw):
    nblk, bs, _ = w.shape
    per = bw // bs
    wt = w.reshape(nblk // per, per, bs, bs)
    eye = jnp.eye(per, dtype=w.dtype)
    return jnp.einsum("npio,pq->npiqo", wt, eye).reshape(nblk // per, bw, bw).astype(MXU_DTYPE)


def _sample_rows(a, lead):
    r = a.shape[1]
    a = jnp.pad(a, ((0, 0), (lead, SAMPLE_ROWS - lead - r), (0, 0)))
    return a.reshape(DEC_BATCH * SAMPLE_ROWS, a.shape[2])


def _conv_tails(proj, proj_s, col0, width):
    tail = CONV_W - 1
    p = jnp.stack([lax.slice(proj, ((b + 1) * SEQ - tail, col0), ((b + 1) * SEQ, col0 + width))
                   for b in range(BATCH)])
    s = lax.slice(proj_s, (0, col0), (proj_s.shape[0], col0 + width))
    s = s.reshape(DEC_BATCH, SAMPLE_ROWS, width)[:, SAMPLE_ROWS - tail:]
    return p, s


def _mlstm_mixer(hn, x, st, w_up, conv_w, conv_b, w_q, w_k, w_v, w_gates, b_gates, skip, norm_w, w_down):
    dm = _dims()
    inner, dh = dm["ml_inner"], dm["ml_dh"]
    c0, n0, m0, conv0 = st
    up = _mm(hn, w_up[None], 0)
    up_s = _pad_sample(up)
    conv_p, conv_s = _conv_tails(up, up_s, 0, inner)

    bw = MXU_DIM if inner % MXU_DIM == 0 else LANES
    wq, wk, wv = (_expand_block_diag(w, bw) for w in (w_q, w_k, w_v))
    wg = jnp.pad(w_gates, ((0, 0), (0, LANES - w_gates.shape[1]))).astype(MXU_DTYPE)
    bg = jnp.pad(b_gates, (0, LANES - b_gates.shape[0])).reshape(1, LANES)
    cb = conv_b.reshape(1, inner)
    cache_rows = _sample_rows(conv0, SAMPLE_PAD - (CONV_W - 1))
    skip2 = skip.reshape(1, inner)
    nw = norm_w.reshape(1, inner)

    chunk = _tile(SEQ, PROMPT_CHUNK, SUBLANES)
    qp, kp, vp, xcp, gp = _ml_pre(up, None, conv_w, cb, wq, wk, wv, wg, bg, sample=False)
    y, c_p, n_p, m_p = _ml_scan(qp, kp, vp, xcp, up, gp, skip2, nw, None, None,
                                nb=BATCH, nc=SEQ // chunk, chunk=chunk, pad=0, row_blk0=0)
    qs, ks, vs, xcs, gs = _ml_pre(up_s, cache_rows, conv_w, cb, wq, wk, wv, wg, bg, sample=True)
    y_s, c_s, n_s, m_s = _ml_scan(qs, ks, vs, xcs, up_s, gs, skip2, nw,
                                  (c0, n0.reshape(DEC_BATCH, ML_HEADS, 1, dh), m0.reshape(-1)), None,
                                  nb=DEC_BATCH, nc=1, chunk=SAMPLE_ROWS, pad=SAMPLE_PAD, row_blk0=0)
    x = _mm(_merge_sample(y, y_s), w_down[None], 0, res=x, scale=1.0, tm_pref=256)
    prompt_state = (c_p, n_p[:, :, 0], m_p[:, :, 0, 0], conv_p)
    sample_state = (c_s, n_s[:, :, 0], m_s[:, :, 0, 0], conv_s)
    return x, prompt_state, sample_state


def _gelu_tanh(x):
    return 0.5 * x * (1.0 + jnp.tanh(0.7978845608028654 * (x + 0.044715 * x * x * x)))


def _one_minus_exp(y):
    series = -y * (1.0 + y * (0.5 + y * (1.0 / 6.0 + y * (1.0 / 24.0 + y * (1.0 / 120.0)))))
    return jnp.where(y > -0.01, series, 1.0 - jnp.exp(y))


def _rg_kernel(gate_ref, xr_ref, halo_ref, *rest, sample, seq_len):
    rest = list(rest)
    if sample:
        cache_ref, h0_ref = rest.pop(0), rest.pop(0)
    (cw_ref, cb_ref, wa_ref, ba_ref, wx_ref, bx_ref, lam_ref, y_ref, h_ref, win_ref, carry_ref) = rest
    tm = xr_ref.shape[0]
    i = pl.program_id(2)
    xr = xr_ref[...]
    row = lax.broadcasted_iota(jnp.int32, (tm, 1), 0)
    if sample:
        tok = _token_rows(tm)
        xr = jnp.where(tok, xr, cache_ref[...])
        halo = jnp.zeros(halo_ref.shape, F32)
    else:
        halo = jnp.where(i == 0, 0.0, halo_ref[...])
    xc = _causal_conv(xr, halo, win_ref, cw_ref, cb_ref)
    r = jax.nn.sigmoid(_dot(xc, wa_ref[...]) + ba_ref[...])
    gi = jax.nn.sigmoid(_dot(xc, wx_ref[...]) + bx_ref[...])
    log_a = -RG_C * r * _softplus(-lam_ref[...])
    a = jnp.exp(log_a)
    bx = jnp.sqrt(_one_minus_exp(2.0 * log_a)) * (gi * xc)
    if sample:
        pos = row % SAMPLE_ROWS
        a = jnp.where(tok, a, 1.0)
        bx = jnp.where(tok, bx, jnp.where(pos == SAMPLE_PAD - 1, h0_ref[...], 0.0))
        _, hid = _linear_scan_rows(a, bx, SAMPLE_ROWS, pos)
        h_ref[...] = hid
        y_ref[...] = jnp.where(tok, hid * _gelu_tanh(gate_ref[...]), 0.0).astype(y_ref.dtype)
    else:
        @pl.when(i == 0)
        def _():
            carry_ref[...] = jnp.zeros(carry_ref.shape, F32)
        a_cum, b_cum = _linear_scan_rows(a, bx, tm, row)
        hid = a_cum * carry_ref[...] + b_cum
        last = hid[tm - 1:tm, :]
        carry_ref[...] = last
        y_ref[...] = (hid * _gelu_tanh(gate_ref[...])).astype(y_ref.dtype)

        @pl.when(i == pl.num_programs(2) - 1)
        def _():
            h_ref[0] = last


def _rg_mix(proj, cache_rows, h0_rows, cw, cb, wa, ba, wx, bx, lam, ybuf, *, sample):
    dm = _dims()
    width, blk, nh = dm["rg_width"], dm["rg_block"], RG_HEADS
    nts = dm["nts"]
    nt = proj.shape[0]
    if sample:
        tm = _tile(nts, PROMPT_CHUNK, SAMPLE_ROWS)
        nb, nc = 1, nts // tm
    else:
        tm = _tile(SEQ, PROMPT_CHUNK, SUBLANES)
        nb, nc = BATCH, SEQ // tm
    hb = tm // HALO

    def rows(b, i):
        return b * nc + i

    in_specs = [pl.BlockSpec((tm, blk), lambda b, j, i: (rows(b, i), j)),
                pl.BlockSpec((tm, blk), lambda b, j, i: (rows(b, i), nh + j)),
                pl.BlockSpec((HALO, blk), lambda b, j, i: (jnp.maximum(rows(b, i) * hb - 1, 0), nh + j))]
    args = [proj, proj, proj]
    if sample:
        in_specs += [pl.BlockSpec((tm, blk), lambda b, j, i: (i, j))] * 2
        args += [cache_rows, h0_rows]
    vec = pl.BlockSpec((1, blk), lambda b, j, i: (0, j))
    mat = pl.BlockSpec((None, blk, blk), lambda b, j, i: (j, 0, 0))
    in_specs += [pl.BlockSpec((CONV_W, blk), lambda b, j, i: (0, j)), vec, mat, vec, mat, vec, vec]
    args += [cw, cb, wa, ba, wx, bx, lam]
    aliases = {}
    if ybuf is not None:
        in_specs.append(pl.BlockSpec(memory_space=pl.ANY))
        args.append(ybuf)
        aliases = {len(args) - 1: 0}
    if sample:
        h_spec = pl.BlockSpec((tm, blk), lambda b, j, i: (i, j))
        h_shape = jax.ShapeDtypeStruct((nts, width), F32)
    else:
        h_spec = pl.BlockSpec((1, 1, blk), lambda b, j, i: (b, 0, j))
        h_shape = jax.ShapeDtypeStruct((BATCH, 1, width), F32)

    def kern(*refs):
        refs = list(refs)
        if ybuf is not None:
            refs.pop(len(args) - 1)
        _rg_kernel(*refs, sample=sample, seq_len=SEQ)

    return pl.pallas_call(
        kern,
        grid=(nb, nh, nc),
        in_specs=in_specs,
        out_specs=[pl.BlockSpec((tm, blk), lambda b, j, i: (rows(b, i), j)), h_spec],
        out_shape=[jax.ShapeDtypeStruct((nt, width), F32), h_shape],
        scratch_shapes=[pltpu.VMEM((HALO + tm, blk), F32), pltpu.VMEM((1, blk), F32)],
        input_output_aliases=aliases,
        compiler_params=_cparams(("parallel", "parallel", "arbitrary")),
        name="rglru_mix",
    )(*args)


def _rglru_mixer(hn, x, st, w_in, conv_w, conv_b, w_a, b_a, w_x, b_x, lam, w_out):
    dm = _dims()
    width = dm["rg_width"]
    h0, conv0 = st
    proj = _mm(hn, w_in[None], 0)
    proj_s = _pad_sample(proj)
    conv_p, conv_s = _conv_tails(proj, proj_s, width, width)
    row = lambda a: a.reshape(1, width)
    cache_rows = _sample_rows(conv0, SAMPLE_PAD - (CONV_W - 1))
    h0_rows = _sample_rows(h0[:, None, :], SAMPLE_PAD - 1)
    consts = (conv_w, row(conv_b), w_a, row(b_a), w_x, row(b_x), row(lam))
    y, h_p = _rg_mix(proj, None, None, *consts, None, sample=False)
    y_s, h_all = _rg_mix(proj_s, cache_rows, h0_rows, *consts, None, sample=True)
    h_s = h_all.reshape(DEC_BATCH, SAMPLE_ROWS, width)[:, SAMPLE_ROWS - 1]
    x = _mm(_merge_sample(y, y_s), w_out[None], 0, res=x, scale=1.0, tm_pref=256)
    return x, (h_p[:, 0], conv_p), (h_s, conv_s)


def _ssd_scan_kernel(*refs, chunk, pad, has_init, has_ybuf, n_heads, rep, headdim):
    refs = list(refs)
    raw = [refs.pop(0) for _ in range(3)]
    halos = [refs.pop(0) for _ in range(3)]
    caches = [refs.pop(0) for _ in range(3)] if has_init else [None] * 3
    cws = [refs.pop(0) for _ in range(3)]
    cbs = [refs.pop(0) for _ in range(3)]
    z_ref, dt_ref, dtb_ref, alog_ref, dskip_ref, nw_ref = [refs.pop(0) for _ in range(6)]
    s0_ref = refs.pop(0) if has_init else None
    if has_ybuf:
        refs.pop(0)
    y_ref, s_out_ref, s_ref = refs[:3]
    wins = refs[3:6]
    acts = refs[6:9]
    xs_ref, bm_ref, cm_ref = acts
    g = pl.program_id(1)
    c = pl.program_id(2)
    gw = rep * headdim
    nseq = s_out_ref.shape[0]

    nrows = raw[0].shape[0]
    for x_ref, halo_ref, cache_ref, cw_ref, cb_ref, win_ref, act_ref in zip(raw, halos, caches, cws, cbs, wins, acts):
        x = x_ref[...]
        if has_init:
            x = jnp.where(_token_rows(nrows), x, cache_ref[...])
            halo = jnp.zeros(halo_ref.shape, F32)
        else:
            halo = jnp.where(c == 0, 0.0, halo_ref[...])
        act_ref[...] = _silu(_causal_conv(x, halo, win_ref, cw_ref, cb_ref))

    t_col = lax.broadcasted_iota(jnp.int32, (chunk, 1), 0)
    valid_col = t_col >= pad
    lane = lax.broadcasted_iota(jnp.int32, (chunk, LANES), 1)
    head_ok = valid_col & (lane < n_heads)
    ci = lax.broadcasted_iota(jnp.int32, (LANES, gw), 0)
    ji = lax.broadcasted_iota(jnp.int32, (LANES, gw), 1)
    spread = (ci == g * rep + ji // headdim).astype(F32)
    ci2 = lax.broadcasted_iota(jnp.int32, (LANES, LANES), 0)
    ri2 = lax.broadcasted_iota(jnp.int32, (LANES, LANES), 1)
    to_lanes = ((ci2 == g * rep + ri2) & (ri2 < rep)).astype(F32)
    r8 = lax.broadcasted_iota(jnp.int32, (SUBLANES, LANES), 0)
    c8 = lax.broadcasted_iota(jnp.int32, (SUBLANES, LANES), 1)
    to_sublanes = (c8 == g * rep + r8).astype(F32)
    t_mat = lax.broadcasted_iota(jnp.int32, (chunk, chunk), 0)
    s_mat = lax.broadcasted_iota(jnp.int32, (chunk, chunk), 1)
    tril = t_mat >= s_mat
    lane_head = lax.broadcasted_iota(jnp.int32, (1, gw), 1) // headdim
    neg_a = -jnp.exp(alog_ref[...])

    def one_seq(i, s):
        rows = slice(i * chunk, (i + 1) * chunk)
        dt = jnp.where(head_ok, _softplus(dt_ref[rows, :] + dtb_ref[...]), 0.0)
        log_a = jnp.where(head_ok, dt * neg_a, 0.0)
        bcum = _cumsum_rows(log_a)
        dt_w = _dot_exact(dt, spread)
        b_w = _dot_exact(bcum, spread)
        b_cols = _dot_exact(bcum, to_lanes)
        b_rows = _dot_nt_exact(to_sublanes, bcum)

        xs = xs_ref[rows, :]
        bm = bm_ref[rows, :]
        cm = cm_ref[rows, :]
        vals = xs * dt_w
        qk = _dot_nt(cm, bm)
        y = jnp.zeros((chunk, gw), F32)
        for r in range(rep):
            d = jnp.where(tril, b_cols[:, r:r + 1] - b_rows[r:r + 1, :], -jnp.inf)
            y = y + _dot(qk * jnp.exp(d), jnp.where(lane_head == r, vals, 0.0))
        b_last = b_w[chunk - 1:chunk, :]
        y = y + jnp.exp(b_w) * _dot_nt(cm, s)
        upd = _dot_tn(vals * jnp.exp(b_last - b_w), bm)
        s_new = jnp.concatenate(
            [jnp.exp(b_rows[r:r + 1, chunk - 1:chunk]) * s[r * headdim:(r + 1) * headdim, :]
             + upd[r * headdim:(r + 1) * headdim, :] for r in range(rep)], axis=0)

        y = (y + dskip_ref[...] * xs) * _silu(z_ref[rows, :])
        yn = y * lax.rsqrt(jnp.mean(y * y, axis=-1, keepdims=True) + EPS) * nw_ref[...]
        y_ref[rows, :] = jnp.where(valid_col, yn, 0.0).astype(y_ref.dtype)
        return s_new

    state_shape = s_out_ref.shape[1:]
    if has_init:
        for i in range(nseq):
            s_out_ref[i] = one_seq(i, s0_ref[i].reshape(s_ref.shape)).reshape(state_shape)
    else:
        @pl.when(c == 0)
        def _():
            s_ref[...] = jnp.zeros(s_ref.shape, F32)

        s_new = one_seq(0, s_ref[...])
        s_ref[...] = s_new

        @pl.when(c == pl.num_programs(2) - 1)
        def _():
            s_out_ref[0] = s_new.reshape(state_shape)


def _ssd_scan(proj, cache_rows, cw, cb, dtb, alog, dskip, norm_w, s0, ybuf, *, nb, nc, chunk, pad, nseq=1):
    dm = _dims()
    inner, nheads, rep = dm["ssd_inner"], dm["ssd_heads"], dm["ssd_rep"]
    assert rep == SUBLANES and nheads <= LANES and SSD_STATE == LANES
    assert nseq == 1 or (nc == 1 and s0 is not None)
    gw = rep * SSD_HEADDIM
    ng = SSD_GROUPS
    nt = proj.shape[0]
    dt_blk = (inner + dm["ssd_conv"]) // LANES
    rb = nseq * chunk
    hb = rb // HALO
    parts = [(gw, lambda g: g), (SSD_STATE, lambda g: inner // SSD_STATE + g),
             (SSD_STATE, lambda g: inner // SSD_STATE + ng + g)]

    def loc(b, c):
        return b * nc + c

    def spec(rows, w, row_fn, col_fn):
        return pl.BlockSpec((rows, w), lambda b, g, c: (row_fn(b, c), col_fn(g)))

    in_specs, args = [], []
    for w, cblk in parts:
        in_specs.append(spec(rb, w, loc, lambda g, w=w, cblk=cblk: inner // w + cblk(g)))
        args.append(proj)
    for w, cblk in parts:
        in_specs.append(spec(HALO, w, lambda b, c: jnp.maximum(loc(b, c) * hb - 1, 0),
                             lambda g, w=w, cblk=cblk: inner // w + cblk(g)))
        args.append(proj)
    if s0 is not None:
        for w, cblk in parts:
            in_specs.append(spec(rb, w, loc, cblk))
            args.append(cache_rows)
    for arr, rows in ((cw, CONV_W), (cb, 1)):
        for w, cblk in parts:
            in_specs.append(spec(rows, w, lambda b, c: 0, cblk))
            args.append(arr)
    in_specs += [pl.BlockSpec((rb, gw), lambda b, g, c: (loc(b, c), g)),
                 pl.BlockSpec((rb, LANES), lambda b, g, c: (loc(b, c), dt_blk)),
                 pl.BlockSpec((1, LANES), lambda b, g, c: (0, 0)),
                 pl.BlockSpec((1, LANES), lambda b, g, c: (0, 0)),
                 pl.BlockSpec((1, gw), lambda b, g, c: (0, g)),
                 pl.BlockSpec((1, gw), lambda b, g, c: (0, g))]
    args += [proj, proj, dtb, alog, dskip, norm_w]
    aliases = {}
    state_spec = pl.BlockSpec((nseq, rep, SSD_HEADDIM, SSD_STATE), lambda b, g, c: (b, g, 0, 0))
    state_shape = jax.ShapeDtypeStruct((nb * nseq, nheads, SSD_HEADDIM, SSD_STATE), F32)
    if s0 is not None:
        in_specs.append(state_spec)
        args.append(s0)
    if ybuf is not None:
        in_specs.append(pl.BlockSpec(memory_space=pl.ANY))
        args.append(ybuf)
        aliases = {len(args) - 1: 0}
    return pl.pallas_call(
        functools.partial(_ssd_scan_kernel, chunk=chunk, pad=pad, has_init=s0 is not None,
                          has_ybuf=ybuf is not None, n_heads=nheads, rep=rep, headdim=SSD_HEADDIM),
        grid=(nb, ng, nc),
        in_specs=in_specs,
        out_specs=[pl.BlockSpec((rb, gw), lambda b, g, c: (loc(b, c), g)), state_spec],
        out_shape=[jax.ShapeDtypeStruct((nt, inner), F32), state_shape],
        scratch_shapes=([pltpu.VMEM((gw, SSD_STATE), F32)]
                        + [pltpu.VMEM((HALO + rb, w), F32) for w, _ in parts]
                        + [pltpu.VMEM((rb, w), F32) for w, _ in parts]),
        input_output_aliases=aliases,
        compiler_params=_cparams(("parallel", "parallel", "arbitrary")),
        name="ssd_scan",
    )(*args)


def _ssd_mixer(hn, x, st, w_in, conv_w, conv_b, dt_bias, a_log, d_skip, norm_w, w_out):
    dm = _dims()
    inner, cdim, nheads = dm["ssd_inner"], dm["ssd_conv"], dm["ssd_heads"]
    s0, conv0 = st
    proj = _mm(hn, w_in[None], 0)
    proj_s = _pad_sample(proj)
    conv_p, conv_s = _conv_tails(proj, proj_s, inner, cdim)
    cb = conv_b.reshape(1, cdim)
    cache_rows = _sample_rows(conv0, SAMPLE_PAD - (CONV_W - 1))
    padl = lambda a: jnp.pad(a, (0, LANES - nheads)).reshape(1, LANES)
    dtb, alog = padl(dt_bias), padl(a_log)
    dskip = jnp.repeat(d_skip, SSD_HEADDIM).reshape(1, inner)
    nw = norm_w.reshape(1, inner)

    chunk = _tile(SEQ, PROMPT_CHUNK, SUBLANES)
    y, s_p = _ssd_scan(proj, None, conv_w, cb, dtb, alog, dskip, nw, None, None,
                       nb=BATCH, nc=SEQ // chunk, chunk=chunk, pad=0)
    nseq = _tile(DEC_BATCH, SAMPLE_SEQS_PER_STEP, 1)
    y_s, s_s = _ssd_scan(proj_s, cache_rows, conv_w, cb, dtb, alog, dskip, nw, jnp.swapaxes(s0, 2, 3), None,
                         nb=DEC_BATCH // nseq, nc=1, chunk=SAMPLE_ROWS, pad=SAMPLE_PAD, nseq=nseq)
    x = _mm(_merge_sample(y, y_s), w_out[None], 0, res=x, scale=1.0, tm_pref=256)
    return x, (jnp.swapaxes(s_p, 2, 3), conv_p), (jnp.swapaxes(s_s, 2, 3), conv_s)


def kernel(x_prompt, x_sample, state_ret, state_mlstm_c, state_mlstm_n, state_mlstm_m, cache_mlstm_conv,
           state_rglru_h, cache_rglru_conv, state_ssd, cache_ssd_conv,
           norm_ffn1, w_ffn1_up, w_ffn1_down, norm_mix, norm_ffn2, w_ffn2_up, w_ffn2_down, norm_final,
           ret_w_in, ret_norm, ret_w_out,
           ml_w_up, ml_conv_w, ml_conv_b, ml_w_q, ml_w_k, ml_w_v, ml_w_gates, ml_b_gates, ml_skip, ml_norm, ml_w_down,
           rg_w_in, rg_conv_w, rg_conv_b, rg_w_a, rg_b_a, rg_w_x, rg_b_x, rg_lambda, rg_w_out,
           ssd_w_in, ssd_conv_w, ssd_conv_b, ssd_dt_bias, ssd_a_log, ssd_d, ssd_norm, ssd_w_out):
    d = D_MODEL
    ntp = BATCH * SEQ
    x = jnp.concatenate([x_prompt.reshape(ntp, d), x_sample.reshape(DEC_BATCH * DEC_SEQ, d)], axis=0)

    n1 = norm_ffn1.reshape(DEPTH, 1, d)
    nm = norm_mix.reshape(DEPTH, 1, d)
    n2 = norm_ffn2.reshape(DEPTH, 1, d)
    states_p = {}
    states_s = {}
    for i in range(DEPTH):
        x = _ffn(x, n1, w_ffn1_up, w_ffn1_down, i)
        hn = _rmsnorm(x, nm, i, MXU_DTYPE)
        kind = i % N_MIXERS
        if kind == 0:
            x, states_p["ret"], states_s["ret"] = _retention_mixer(hn, x, state_ret, ret_w_in, ret_norm, ret_w_out)
        elif kind == 1:
            x, states_p["ml"], states_s["ml"] = _mlstm_mixer(
                hn, x, (state_mlstm_c, state_mlstm_n, state_mlstm_m, cache_mlstm_conv),
                ml_w_up, ml_conv_w, ml_conv_b, ml_w_q, ml_w_k, ml_w_v, ml_w_gates, ml_b_gates, ml_skip, ml_norm,
                ml_w_down)
        elif kind == 2:
            x, states_p["rg"], states_s["rg"] = _rglru_mixer(
                hn, x, (state_rglru_h, cache_rglru_conv), rg_w_in, rg_conv_w, rg_conv_b, rg_w_a, rg_b_a,
                rg_w_x, rg_b_x, rg_lambda, rg_w_out)
        else:
            x, states_p["ssd"], states_s["ssd"] = _ssd_mixer(
                hn, x, (state_ssd, cache_ssd_conv), ssd_w_in, ssd_conv_w, ssd_conv_b, ssd_dt_bias, ssd_a_log,
                ssd_d, ssd_norm, ssd_w_out)
        x = _ffn(x, n2, w_ffn2_up, w_ffn2_down, i)
    y = _rmsnorm(x, norm_final.reshape(1, 1, d), 0, F32)
    y_prompt = y[:ntp].reshape(BATCH, SEQ, d)
    y_sample = y[ntp:].reshape(DEC_BATCH, DEC_SEQ, d)

    def flat(st):
        return (st["ret"],) + tuple(st["ml"]) + tuple(st["rg"]) + tuple(st["ssd"])

    return (y_prompt, y_sample) + flat(states_p) + flat(states_s)
```

```python
import functools

import jax
import jax.numpy as jnp
from jax import lax
from jax.experimental import pallas as pl
from jax.experimental.pallas import tpu as pltpu

D_MODEL = 2048
BATCH = 4
SEQ = 2048
DEPTH = 4
DEC_BATCH = 128
DEC_SEQ = 4
PAST_LEN = 16384
N_MIXERS = 4
D_FF = 5632
EPS = 1e-6
CONV_W = 4

RET_HEADS = 8
ROPE_BASE = 10000.0
ML_HEADS = 4
ML_QKV_BLOCK = 4
RG_HEADS = 10
RG_C = 8.0
SSD_HEADDIM = 64
SSD_GROUPS = 8
SSD_STATE = 128

F32 = jnp.float32
MXU_DTYPE = jnp.bfloat16

SUBLANES = 8
LANES = 128
MXU_DIM = 256
VMEM_LIMIT = 56 * 1024 * 1024

SAMPLE_ROWS = SUBLANES
SAMPLE_PAD = SAMPLE_ROWS - DEC_SEQ
HALO = SUBLANES

ROW_TILE = 1088
NORM_ROW_TILE = 512
COL_TILE = 512
WIDE_COL_TILE = 1024
SAMPLE_SEQS_PER_STEP = 4
PROMPT_CHUNK = 256
PRE_ROW_TILE = 128


def _dims():
    d = D_MODEL
    ret_dk = d // RET_HEADS
    ml_inner = 2 * d
    rg_width = d * 5 // 4
    ssd_inner = 2 * d
    ssd_heads = ssd_inner // SSD_HEADDIM
    return dict(
        ret_dk=ret_dk, ret_dv=2 * ret_dk,
        ml_inner=ml_inner, ml_dh=ml_inner // ML_HEADS,
        rg_width=rg_width, rg_block=rg_width // RG_HEADS,
        ssd_inner=ssd_inner, ssd_heads=ssd_heads, ssd_rep=ssd_heads // SSD_GROUPS,
        ssd_conv=ssd_inner + 2 * SSD_GROUPS * SSD_STATE,
        ntp=BATCH * SEQ, nts=DEC_BATCH * SAMPLE_ROWS,
    )


def _tile(n, pref, mult):
    t = min(pref, n) // mult * mult
    while t > mult and n % t:
        t -= mult
    assert t >= mult and n % t == 0, (n, pref, mult)
    return t


def _cparams(sem):
    return pltpu.CompilerParams(dimension_semantics=sem, vmem_limit_bytes=VMEM_LIMIT)


def _silu(x):
    return x * jax.nn.sigmoid(x)


def _softplus(x):
    return jnp.maximum(x, 0.0) + jnp.log1p(jnp.exp(-jnp.abs(x)))


def _dot(a, b):
    return jnp.dot(a.astype(MXU_DTYPE), b.astype(MXU_DTYPE), preferred_element_type=F32)


def _dot_nt(a, b):
    return lax.dot_general(a.astype(MXU_DTYPE), b.astype(MXU_DTYPE), (((1,), (1,)), ((), ())),
                           preferred_element_type=F32)


def _dot_tn(a, b):
    return lax.dot_general(a.astype(MXU_DTYPE), b.astype(MXU_DTYPE), (((0,), (0,)), ((), ())),
                           preferred_element_type=F32)


def _split3(x):
    hi = x.astype(jnp.bfloat16)
    r1 = x - hi.astype(F32)
    mid = r1.astype(jnp.bfloat16)
    lo = (r1 - mid.astype(F32)).astype(jnp.bfloat16)
    return hi, mid, lo


def _dot_exact(a, sel):
    sel = sel.astype(jnp.bfloat16)
    hi, mid, lo = (jnp.dot(p, sel, preferred_element_type=F32) for p in _split3(a))
    return (hi + mid) + lo


def _dot_nt_exact(sel, b):
    sel = sel.astype(jnp.bfloat16)
    dims = (((1,), (1,)), ((), ()))
    hi, mid, lo = (lax.dot_general(sel, p, dims, preferred_element_type=F32) for p in _split3(b))
    return (hi + mid) + lo


def _cumsum_rows(x):
    n = x.shape[0]
    row = lax.broadcasted_iota(jnp.int32, x.shape, 0)
    sh = 1
    while sh < n:
        x = x + jnp.where(row >= sh, pltpu.roll(x, sh, 0), 0.0)
        sh *= 2
    return x


def _linear_scan_rows(a, b, span, pos):
    sh = 1
    while sh < span:
        ok = pos >= sh
        a_prev = jnp.where(ok, pltpu.roll(a, sh, 0), 1.0)
        b_prev = jnp.where(ok, pltpu.roll(b, sh, 0), 0.0)
        b = b + a * b_prev
        a = a * a_prev
        sh *= 2
    return a, b


def _causal_conv(x, halo, win_ref, w_ref, b_ref):
    tm = x.shape[0]
    win_ref[0:HALO, :] = halo
    win_ref[HALO:HALO + tm, :] = x
    y = b_ref[...] + x * w_ref[CONV_W - 1:CONV_W, :]
    for j in range(1, CONV_W):
        y = y + win_ref[pl.ds(HALO - j, tm), :] * w_ref[CONV_W - 1 - j:CONV_W - j, :]
    return y


def _mixer_out_dtype(block_rows):
    packed_rows = SUBLANES * (4 // jnp.dtype(MXU_DTYPE).itemsize)
    return MXU_DTYPE if block_rows % packed_rows == 0 else F32


def _token_rows(n):
    row = lax.broadcasted_iota(jnp.int32, (n, 1), 0)
    return (row % SAMPLE_ROWS) >= SAMPLE_PAD


def _rmsnorm_kernel(x_ref, w_ref, o_ref):
    x = x_ref[...]
    ms = jnp.mean(x * x, axis=-1, keepdims=True)
    o_ref[...] = (x * lax.rsqrt(ms + EPS) * w_ref[...]).astype(o_ref.dtype)


def _rmsnorm(x, w, layer, out_dtype):
    m, d = x.shape
    tm = _tile(m, NORM_ROW_TILE, 16)
    return pl.pallas_call(
        _rmsnorm_kernel,
        grid=(m // tm,),
        in_specs=[pl.BlockSpec((tm, d), lambda i: (i, 0)),
                  pl.BlockSpec((None, 1, d), lambda i: (layer, 0, 0))],
        out_specs=pl.BlockSpec((tm, d), lambda i: (i, 0)),
        out_shape=jax.ShapeDtypeStruct((m, d), out_dtype),
        compiler_params=_cparams(("parallel",)),
        name="rmsnorm",
    )(x, w)


def _mm_kernel(a_ref, w_ref, *rest, scale, has_res):
    if has_res:
        r_ref, o_ref, wb_ref = rest
    else:
        o_ref, wb_ref = rest

    @pl.when(pl.program_id(1) == 0)
    def _():
        wb_ref[...] = w_ref[...].astype(MXU_DTYPE)

    acc = jnp.dot(a_ref[...].astype(MXU_DTYPE), wb_ref[...], preferred_element_type=F32)
    if has_res:
        acc = r_ref[...] + scale * acc
    o_ref[...] = acc.astype(o_ref.dtype)


def _mm(a, w, layer, *, res=None, scale=1.0, tm_pref=ROW_TILE, tn_pref=WIDE_COL_TILE):
    m, k = a.shape
    n = w.shape[2]
    tm = _tile(m, tm_pref, 16)
    tn = min(tn_pref, n)
    nj = pl.cdiv(n, tn)
    w_mode = dict(pipeline_mode=pl.Buffered(1)) if res is not None else {}
    in_specs = [pl.BlockSpec((tm, k), lambda j, i: (i, 0)),
                pl.BlockSpec((None, k, tn), lambda j, i: (layer, 0, j), **w_mode)]
    args = [a, w]
    aliases = {}
    if res is not None:
        assert n % tn == 0 and res.shape == (m, n)
        in_specs.append(pl.BlockSpec((tm, tn), lambda j, i: (i, j)))
        args.append(res)
        aliases = {2: 0}
    return pl.pallas_call(
        functools.partial(_mm_kernel, scale=scale, has_res=res is not None),
        grid=(nj, m // tm),
        in_specs=in_specs,
        out_specs=pl.BlockSpec((tm, tn), lambda j, i: (i, j)),
        out_shape=jax.ShapeDtypeStruct((m, nj * tn), F32),
        scratch_shapes=[pltpu.VMEM((k, tn), MXU_DTYPE)],
        input_output_aliases=aliases,
        compiler_params=_cparams(("parallel", "arbitrary")),
        name="matmul",
    )(*args)


def _swiglu_up_kernel(a_ref, wg_ref, wu_ref, o_ref, wgb_ref, wub_ref):
    @pl.when(pl.program_id(1) == 0)
    def _():
        wgb_ref[...] = wg_ref[...].astype(MXU_DTYPE)
        wub_ref[...] = wu_ref[...].astype(MXU_DTYPE)

    a = a_ref[...]
    gate = jnp.dot(a, wgb_ref[...], preferred_element_type=F32)
    up = jnp.dot(a, wub_ref[...], preferred_element_type=F32)
    o_ref[...] = (_silu(gate) * up).astype(o_ref.dtype)


def _swiglu_up(a, w_up, layer):
    m, k = a.shape
    ff = w_up.shape[2] // 2
    tm = _tile(m, ROW_TILE, 16)
    tn = _tile(ff, COL_TILE, LANES)
    nj = ff // tn
    return pl.pallas_call(
        _swiglu_up_kernel,
        grid=(nj, m // tm),
        in_specs=[pl.BlockSpec((tm, k), lambda j, i: (i, 0)),
                  pl.BlockSpec((None, k, tn), lambda j, i: (layer, 0, j)),
                  pl.BlockSpec((None, k, tn), lambda j, i: (layer, 0, nj + j))],
        out_specs=pl.BlockSpec((tm, tn), lambda j, i: (i, j)),
        out_shape=jax.ShapeDtypeStruct((m, ff), MXU_DTYPE),
        scratch_shapes=[pltpu.VMEM((k, tn), MXU_DTYPE), pltpu.VMEM((k, tn), MXU_DTYPE)],
        compiler_params=_cparams(("parallel", "arbitrary")),
        name="swiglu_up",
    )(a, w_up, w_up)


def _ffn(x, norm_w, w_up, w_down, layer):
    h = _rmsnorm(x, norm_w, layer, MXU_DTYPE)
    act = _swiglu_up(h, w_up, layer)
    return _mm(act, w_down, layer, res=x, scale=0.5, tm_pref=256)


def _ret_scan_kernel(lg_ref, q_ref, k_ref, v_ref, g_ref, cos_ref, sin_ref, nw_ref, *rest,
                     chunk, pad, has_init, k_scale):
    rest = list(rest)
    s0_ref = rest.pop(0) if has_init else None
    y_ref, s_out_ref, s_ref = rest
    h = pl.program_id(1)
    c = pl.program_id(2)
    nvalid = chunk - pad
    nseq = s_out_ref.shape[0]
    lam = lg_ref[h]
    half = cos_ref.shape[1]

    t_col = lax.broadcasted_iota(jnp.int32, (chunk, 1), 0)
    cnt_col = jnp.maximum(t_col - pad + 1, 0).astype(F32)
    t_mat = lax.broadcasted_iota(jnp.int32, (chunk, chunk), 0)
    s_mat = lax.broadcasted_iota(jnp.int32, (chunk, chunk), 1)
    cnt_t = jnp.maximum(t_mat - pad + 1, 0).astype(F32)
    cnt_s = jnp.maximum(s_mat - pad + 1, 0).astype(F32)
    w = jnp.where((t_mat >= s_mat) & (s_mat >= pad), jnp.exp(lam * (cnt_t - cnt_s)), 0.0)
    ws = jnp.where(t_col >= pad, jnp.exp(lam * (nvalid - cnt_col)), 0.0)
    w_in = jnp.exp(lam * cnt_col)
    w_state = jnp.exp(lam * nvalid)

    def one_seq(i, s):
        r = slice(i * chunk, (i + 1) * chunk)
        cos = cos_ref[r, :]
        sin = sin_ref[r, :]

        def rot(x):
            x1 = x[:, :half]
            x2 = x[:, half:]
            return jnp.concatenate([x1 * cos - x2 * sin, x1 * sin + x2 * cos], axis=-1)

        q = rot(q_ref[r, :])
        k = rot(k_ref[r, :]) * k_scale
        v = v_ref[r, :]
        y = _dot(_dot_nt(q, k) * w, v) + w_in * _dot(q, s)
        s_new = w_state * s + _dot_tn(k * ws, v)
        mu = jnp.mean(y, axis=-1, keepdims=True)
        yc = y - mu
        yn = yc * lax.rsqrt(jnp.mean(yc * yc, axis=-1, keepdims=True) + EPS) * nw_ref[...]
        out = _silu(g_ref[r, :]) * yn
        y_ref[r, :] = jnp.where(t_col >= pad, out, 0.0).astype(y_ref.dtype)
        return s_new

    if has_init:
        for i in range(nseq):
            s_out_ref[i, 0] = one_seq(i, s0_ref[i, 0])
    else:
        @pl.when(c == 0)
        def _():
            s_ref[...] = jnp.zeros(s_ref.shape, F32)

        s_new = one_seq(0, s_ref[...])
        s_ref[...] = s_new

        @pl.when(c == pl.num_programs(2) - 1)
        def _():
            s_out_ref[0, 0] = s_new


def _ret_scan(proj, cos, sin, log_gamma, norm_w, s0, *, nb, nc, chunk, pad, nseq=1):
    dm = _dims()
    dk, dv, nh = dm["ret_dk"], dm["ret_dv"], RET_HEADS
    nt = proj.shape[0]
    assert nseq == 1 or (nc == 1 and s0 is not None)
    rb = nseq * chunk
    y_dtype = _mixer_out_dtype(chunk)

    def rows(b, c):
        return b * nc + c

    in_specs = [pl.BlockSpec(memory_space=pltpu.SMEM),
                pl.BlockSpec((rb, dk), lambda b, h, c: (rows(b, c), h)),
                pl.BlockSpec((rb, dk), lambda b, h, c: (rows(b, c), nh + h)),
                pl.BlockSpec((rb, dv), lambda b, h, c: (rows(b, c), nh + h)),
                pl.BlockSpec((rb, dv), lambda b, h, c: (rows(b, c), 2 * nh + h)),
                pl.BlockSpec((rb, dk // 2), lambda b, h, c: (rows(b, c), 0)),
                pl.BlockSpec((rb, dk // 2), lambda b, h, c: (rows(b, c), 0)),
                pl.BlockSpec((1, dv), lambda b, h, c: (0, h))]
    args = [log_gamma, proj, proj, proj, proj, cos, sin, norm_w]
    if s0 is not None:
        in_specs.append(pl.BlockSpec((nseq, 1, dk, dv), lambda b, h, c: (b, h, 0, 0)))
        args.append(s0)
    return pl.pallas_call(
        functools.partial(_ret_scan_kernel, chunk=chunk, pad=pad, has_init=s0 is not None, k_scale=dk ** -0.5),
        grid=(nb, nh, nc),
        in_specs=in_specs,
        out_specs=[pl.BlockSpec((rb, dv), lambda b, h, c: (rows(b, c), h)),
                   pl.BlockSpec((nseq, 1, dk, dv), lambda b, h, c: (b, h, 0, 0))],
        out_shape=[jax.ShapeDtypeStruct((nt, nh * dv), y_dtype),
                   jax.ShapeDtypeStruct((nb * nseq, nh, dk, dv), F32)],
        scratch_shapes=[pltpu.VMEM((dk, dv), F32)],
        compiler_params=_cparams(("parallel", "parallel", "arbitrary")),
        name="retention_scan",
    )(*args)


def _pad_sample(a):
    ntp = BATCH * SEQ
    s = lax.slice(a, (ntp, 0), (ntp + DEC_BATCH * DEC_SEQ, a.shape[1]))
    s = jnp.pad(s.reshape(DEC_BATCH, DEC_SEQ, a.shape[1]), ((0, 0), (SAMPLE_PAD, 0), (0, 0)))
    return s.reshape(DEC_BATCH * SAMPLE_ROWS, a.shape[1])


def _merge_sample(y, y_s):
    ntp = BATCH * SEQ
    tok = y_s.reshape(DEC_BATCH, SAMPLE_ROWS, y_s.shape[1])[:, SAMPLE_PAD:].astype(y.dtype)
    return lax.dynamic_update_slice(y, tok.reshape(DEC_BATCH * DEC_SEQ, y_s.shape[1]), (ntp, 0))


def _retention_mixer(hn, x, state_ret, w_in, norm_w, w_out):
    dm = _dims()
    dk = dm["ret_dk"]
    proj = _mm(hn, w_in[None], 0)
    proj_s = _pad_sample(proj)

    half = dk // 2
    inv = ROPE_BASE ** (-jnp.arange(half, dtype=F32) / half)

    def tables(pos):
        ang = pos.astype(F32)[:, None] * inv
        return jnp.cos(ang), jnp.sin(ang)

    cos_p, sin_p = tables(jnp.tile(jnp.arange(SEQ), BATCH))
    cos_s, sin_s = tables(jnp.tile(PAST_LEN - SAMPLE_PAD + jnp.arange(SAMPLE_ROWS), DEC_BATCH))
    log_gamma = jnp.log1p(-(2.0 ** (-5.0 - jnp.arange(RET_HEADS, dtype=F32))))
    nw = norm_w.reshape(1, -1)

    chunk = _tile(SEQ, PROMPT_CHUNK, SUBLANES)
    y, s_p = _ret_scan(proj, cos_p, sin_p, log_gamma, nw, None,
                       nb=BATCH, nc=SEQ // chunk, chunk=chunk, pad=0)
    nseq = _tile(DEC_BATCH, SAMPLE_SEQS_PER_STEP, 1)
    y_s, s_s = _ret_scan(proj_s, cos_s, sin_s, log_gamma, nw, state_ret,
                         nb=DEC_BATCH // nseq, nc=1, chunk=SAMPLE_ROWS, pad=SAMPLE_PAD, nseq=nseq)
    x = _mm(_merge_sample(y, y_s), w_out[None], 0, res=x, scale=1.0, tm_pref=256)
    return x, s_p, s_s


def _ml_pre_kernel(xm_ref, halo_ref, *rest, sample, seq_len, k_scale, nblk, bw):
    rest = list(rest)
    cache_ref = rest.pop(0) if sample else None
    (cw_ref, cb_ref, wq_ref, wk_ref, wv_ref, wg_ref, bg_ref,
     q_ref, k_ref, v_ref, xc_ref, gates_ref, win_ref) = rest
    tm = xm_ref.shape[0]
    xm = xm_ref[...]
    if sample:
        xm = jnp.where(_token_rows(tm), xm, cache_ref[...])
        halo = jnp.zeros(halo_ref.shape, F32)
    else:
        seq_start = (pl.program_id(0) * tm) % seq_len == 0
        halo = jnp.where(seq_start, 0.0, halo_ref[...])
    xc = _silu(_causal_conv(xm, halo, win_ref, cw_ref, cb_ref))
    xc_ref[...] = xc

    for j in range(nblk):
        sl = slice(j * bw, (j + 1) * bw)
        q_ref[:, sl] = _dot(xc[:, sl], wq_ref[j]).astype(q_ref.dtype)
        k_ref[:, sl] = _dot(xc[:, sl], wk_ref[j]).astype(k_ref.dtype)
        v_ref[:, sl] = _dot(xm[:, sl], wv_ref[j]).astype(v_ref.dtype)
    inner = q_ref.shape[1]
    g = (_dot(q_ref[...], wg_ref[0:inner, :]) + _dot(k_ref[...], wg_ref[inner:2 * inner, :])
         + _dot(v_ref[...], wg_ref[2 * inner:3 * inner, :]) + bg_ref[...])
    k_ref[...] = (k_ref[...] * k_scale).astype(k_ref.dtype)
    lane = lax.broadcasted_iota(jnp.int32, g.shape, 1)
    log_f = jnp.minimum(g, 0.0) - jnp.log1p(jnp.exp(-jnp.abs(g)))
    gates_ref[...] = jnp.where(lane < ML_HEADS, g, jnp.where(lane < 2 * ML_HEADS, log_f, 0.0))


def _ml_pre(up, cache_rows, cw, cb, wq, wk, wv, wg, bg, *, sample, chunk):
    dm = _dims()
    inner, dh = dm["ml_inner"], dm["ml_dh"]
    n = dm["nts"] if sample else dm["ntp"]
    tm = _tile(n if sample else SEQ, PRE_ROW_TILE, SAMPLE_ROWS)
    hb = tm // HALO
    nblk, bw = wq.shape[0], wq.shape[1]

    in_specs = [pl.BlockSpec((tm, inner), lambda i: (i, 0)),
                pl.BlockSpec((HALO, inner), lambda i: (jnp.maximum(i * hb - 1, 0), 0))]
    args = [up, up]
    if sample:
        in_specs.append(pl.BlockSpec((tm, inner), lambda i: (i, 0)))
        args.append(cache_rows)
    const2 = lambda i: (0, 0)
    const3 = lambda i: (0, 0, 0)
    in_specs += [pl.BlockSpec(cw.shape, const2), pl.BlockSpec(cb.shape, const2),
                 pl.BlockSpec(wq.shape, const3), pl.BlockSpec(wk.shape, const3), pl.BlockSpec(wv.shape, const3),
                 pl.BlockSpec(wg.shape, const2), pl.BlockSpec(bg.shape, const2)]
    args += [cw, cb, wq, wk, wv, wg, bg]
    row_spec = pl.BlockSpec((tm, inner), lambda i: (i, 0))
    return pl.pallas_call(
        functools.partial(_ml_pre_kernel, sample=sample, seq_len=SEQ, k_scale=dh ** -0.5, nblk=nblk, bw=bw),
        grid=(n // tm,),
        in_specs=in_specs,
        out_specs=[row_spec, row_spec, row_spec, row_spec, pl.BlockSpec((tm, LANES), lambda i: (i, 0))],
        out_shape=([jax.ShapeDtypeStruct((n, inner), _mixer_out_dtype(chunk))] * 3
                   + [jax.ShapeDtypeStruct((n, inner), F32), jax.ShapeDtypeStruct((n, LANES), F32)]),
        scratch_shapes=[pltpu.VMEM((HALO + tm, inner), F32)],
        compiler_params=_cparams(("arbitrary",)),
        name="mlstm_pre",
    )(*args)


def _ml_scan_kernel(m0_ref, q_ref, k_ref, v_ref, xc_ref, z_ref, gates_ref, skip_ref, nw_ref, *rest,
                    chunk, pad, has_init):
    rest = list(rest)
    if has_init:
        c0_ref, n0_ref = rest.pop(0), rest.pop(0)
    y_ref, c_out_ref, n_out_ref, m_out_ref, c_ref, n_ref, m_ref = rest
    b = pl.program_id(0)
    h = pl.program_id(1)
    c = pl.program_id(2)

    @pl.when(c == 0)
    def _():
        if has_init:
            c_ref[...] = c0_ref[0, 0]
            n_ref[...] = n0_ref[0, 0]
            m_ref[...] = jnp.full(m_ref.shape, m0_ref[b * ML_HEADS + h], F32)
        else:
            c_ref[...] = jnp.zeros(c_ref.shape, F32)
            n_ref[...] = jnp.zeros(n_ref.shape, F32)
            m_ref[...] = jnp.zeros(m_ref.shape, F32)

    q = q_ref[...]
    k = k_ref[...]
    v = v_ref[...]
    t_col = lax.broadcasted_iota(jnp.int32, (chunk, 1), 0)
    valid_col = t_col >= pad

    gates = jnp.where(valid_col, gates_ref[...], 0.0)
    csum = _cumsum_rows(gates)
    lane = lax.broadcasted_iota(jnp.int32, gates.shape, 1)
    ig_col = jnp.sum(jnp.where(lane == h, gates, 0.0), axis=1, keepdims=True)
    b_col = jnp.sum(jnp.where(lane == ML_HEADS + h, csum, 0.0), axis=1, keepdims=True)
    ig_col = jnp.where(valid_col, ig_col, -jnp.inf)
    sel = (lax.broadcasted_iota(jnp.int32, (SUBLANES, LANES), 0)
           == lax.broadcasted_iota(jnp.int32, (SUBLANES, LANES), 1)).astype(F32)
    sub = lax.broadcasted_iota(jnp.int32, (SUBLANES, chunk), 0)
    ig_row = jnp.sum(jnp.where(sub == h, _dot_nt_exact(sel, gates), 0.0), axis=0, keepdims=True)
    b_row = jnp.sum(jnp.where(sub == ML_HEADS + h, _dot_nt_exact(sel, csum), 0.0), axis=0, keepdims=True)
    s_row = lax.broadcasted_iota(jnp.int32, (1, chunk), 1)
    ig_row = jnp.where(s_row >= pad, ig_row, -jnp.inf)

    t_mat = lax.broadcasted_iota(jnp.int32, (chunk, chunk), 0)
    s_mat = lax.broadcasted_iota(jnp.int32, (chunk, chunk), 1)
    d = jnp.where(t_mat >= s_mat, b_col - b_row + ig_row, -jnp.inf)
    m_prev = m_ref[...][:, 0:1]
    inter = b_col + m_prev
    m_t = jnp.maximum(inter, jnp.max(d, axis=1, keepdims=True))
    sc = _dot_nt(q, k) * jnp.exp(d - m_t)
    w_inter = jnp.exp(inter - m_t)
    cmat = c_ref[...]
    nvec = n_ref[...]
    num = _dot(sc, v) + w_inter * _dot(q, cmat)
    den = jnp.sum(sc, axis=1, keepdims=True) + w_inter * jnp.sum(q * nvec, axis=1, keepdims=True)
    hid = num / jnp.maximum(jnp.abs(den), jnp.exp(-m_t))

    m_new = m_t[chunk - 1:chunk, :]
    b_last = b_col[chunk - 1:chunk, :]
    ws = jnp.exp(b_last - b_col + ig_col - m_new)
    decay = jnp.exp(b_last + m_prev - m_new)
    kw = k * ws
    c_new = decay * cmat + _dot_tn(kw, v)
    n_new = decay * nvec + jnp.sum(kw, axis=0, keepdims=True)
    c_ref[...] = c_new
    n_ref[...] = n_new
    m_ref[...] = jnp.broadcast_to(m_new, m_ref.shape)

    mu = jnp.mean(hid, axis=-1, keepdims=True)
    hc = hid - mu
    hn = hc * lax.rsqrt(jnp.mean(hc * hc, axis=-1, keepdims=True) + EPS) * nw_ref[...]
    out = (hn + skip_ref[...] * xc_ref[...]) * _silu(z_ref[...])
    y_ref[...] = jnp.where(valid_col, out, 0.0).astype(y_ref.dtype)

    @pl.when(c == pl.num_programs(2) - 1)
    def _():
        c_out_ref[0, 0] = c_new
        n_out_ref[0, 0] = n_new
        m_out_ref[0, 0] = jnp.broadcast_to(m_new, (1, LANES))


def _ml_scan(q, k, v, xc, up, gates, skip, norm_w, init, *, nb, nc, chunk, pad):
    dm = _dims()
    dh, nh, inner = dm["ml_dh"], ML_HEADS, dm["ml_inner"]
    nt = up.shape[0]

    def loc(b, h, c):
        return (b * nc + c, h)

    head_spec = pl.BlockSpec((chunk, dh), loc)
    in_specs = [pl.BlockSpec(memory_space=pltpu.SMEM),
                head_spec, head_spec, head_spec, head_spec,
                pl.BlockSpec((chunk, dh), lambda b, h, c: (b * nc + c, nh + h)),
                pl.BlockSpec((chunk, LANES), lambda b, h, c: (b * nc + c, 0)),
                pl.BlockSpec((1, dh), lambda b, h, c: (0, h)),
                pl.BlockSpec((1, dh), lambda b, h, c: (0, h))]
    has_init = init is not None
    if has_init:
        c0, n0, m0 = init
    else:
        c0 = n0 = None
        m0 = jnp.zeros((1,), F32)
    args = [m0, q, k, v, xc, up, gates, skip, norm_w]
    if has_init:
        in_specs += [pl.BlockSpec((1, 1, dh, dh), lambda b, h, c: (b, h, 0, 0)),
                     pl.BlockSpec((1, 1, 1, dh), lambda b, h, c: (b, h, 0, 0))]
        args += [c0, n0]
    return pl.pallas_call(
        functools.partial(_ml_scan_kernel, chunk=chunk, pad=pad, has_init=has_init),
        grid=(nb, nh, nc),
        in_specs=in_specs,
        out_specs=[pl.BlockSpec((chunk, dh), loc),
                   pl.BlockSpec((1, 1, dh, dh), lambda b, h, c: (b, h, 0, 0)),
                   pl.BlockSpec((1, 1, 1, dh), lambda b, h, c: (b, h, 0, 0)),
                   pl.BlockSpec((1, 1, 1, LANES), lambda b, h, c: (b, h, 0, 0))],
        out_shape=[jax.ShapeDtypeStruct((nt, inner), _mixer_out_dtype(chunk)),
                   jax.ShapeDtypeStruct((nb, nh, dh, dh), F32),
                   jax.ShapeDtypeStruct((nb, nh, 1, dh), F32),
                   jax.ShapeDtypeStruct((nb, nh, 1, LANES), F32)],
        scratch_shapes=[pltpu.VMEM((dh, dh), F32), pltpu.VMEM((1, dh), F32), pltpu.VMEM((1, LANES), F32)],
        compiler_params=_cparams(("parallel", "parallel", "arbitrary")),
        name="mlstm_scan",
    )(*args)


def _expand_block_diag(w, bw):
    nblk, bs, _ = w.shape
    per = bw // bs
    wt = w.reshape(nblk // per, per, bs, bs)
    eye = jnp.eye(per, dtype=w.dtype)
    return jnp.einsum("npio,pq->npiqo", wt, eye).reshape(nblk // per, bw, bw).astype(MXU_DTYPE)


def _sample_rows(a, lead):
    r = a.shape[1]
    a = jnp.pad(a, ((0, 0), (lead, SAMPLE_ROWS - lead - r), (0, 0)))
    return a.reshape(DEC_BATCH * SAMPLE_ROWS, a.shape[2])


def _conv_tails(proj, proj_s, col0, width):
    tail = CONV_W - 1
    p = jnp.stack([lax.slice(proj, ((b + 1) * SEQ - tail, col0), ((b + 1) * SEQ, col0 + width))
                   for b in range(BATCH)])
    s = lax.slice(proj_s, (0, col0), (proj_s.shape[0], col0 + width))
    s = s.reshape(DEC_BATCH, SAMPLE_ROWS, width)[:, SAMPLE_ROWS - tail:]
    return p, s


def _mlstm_mixer(hn, x, st, w_up, conv_w, conv_b, w_q, w_k, w_v, w_gates, b_gates, skip, norm_w, w_down):
    dm = _dims()
    inner, dh = dm["ml_inner"], dm["ml_dh"]
    c0, n0, m0, conv0 = st
    up = _mm(hn, w_up[None], 0)
    up_s = _pad_sample(up)
    conv_p, conv_s = _conv_tails(up, up_s, 0, inner)

    bw = MXU_DIM if inner % MXU_DIM == 0 else LANES
    wq, wk, wv = (_expand_block_diag(w, bw) for w in (w_q, w_k, w_v))
    wg = jnp.pad(w_gates, ((0, 0), (0, LANES - w_gates.shape[1]))).astype(MXU_DTYPE)
    bg = jnp.pad(b_gates, (0, LANES - b_gates.shape[0])).reshape(1, LANES)
    cb = conv_b.reshape(1, inner)
    cache_rows = _sample_rows(conv0, SAMPLE_PAD - (CONV_W - 1))
    skip2 = skip.reshape(1, inner)
    nw = norm_w.reshape(1, inner)

    chunk = _tile(SEQ, PROMPT_CHUNK, SUBLANES)
    qp, kp, vp, xcp, gp = _ml_pre(up, None, conv_w, cb, wq, wk, wv, wg, bg, sample=False, chunk=chunk)
    y, c_p, n_p, m_p = _ml_scan(qp, kp, vp, xcp, up, gp, skip2, nw, None,
                                nb=BATCH, nc=SEQ // chunk, chunk=chunk, pad=0)
    qs, ks, vs, xcs, gs = _ml_pre(up_s, cache_rows, conv_w, cb, wq, wk, wv, wg, bg, sample=True,
                                  chunk=SAMPLE_ROWS)
    y_s, c_s, n_s, m_s = _ml_scan(qs, ks, vs, xcs, up_s, gs, skip2, nw,
                                  (c0, n0.reshape(DEC_BATCH, ML_HEADS, 1, dh), m0.reshape(-1)),
                                  nb=DEC_BATCH, nc=1, chunk=SAMPLE_ROWS, pad=SAMPLE_PAD)
    x = _mm(_merge_sample(y, y_s), w_down[None], 0, res=x, scale=1.0, tm_pref=256)
    prompt_state = (c_p, n_p[:, :, 0], m_p[:, :, 0, 0], conv_p)
    sample_state = (c_s, n_s[:, :, 0], m_s[:, :, 0, 0], conv_s)
    return x, prompt_state, sample_state


def _gelu_tanh(x):
    return 0.5 * x * (1.0 + jnp.tanh(0.7978845608028654 * (x + 0.044715 * x * x * x)))


def _one_minus_exp(y):
    series = -y * (1.0 + y * (0.5 + y * (1.0 / 6.0 + y * (1.0 / 24.0 + y * (1.0 / 120.0)))))
    return jnp.where(y > -0.01, series, 1.0 - jnp.exp(y))


def _rg_kernel(gate_ref, xr_ref, halo_ref, *rest, sample):
    rest = list(rest)
    if sample:
        cache_ref, h0_ref = rest.pop(0), rest.pop(0)
    (cw_ref, cb_ref, wa_ref, ba_ref, wx_ref, bx_ref, lam_ref, y_ref, h_ref, win_ref, carry_ref) = rest
    tm = xr_ref.shape[0]
    i = pl.program_id(2)
    xr = xr_ref[...]
    row = lax.broadcasted_iota(jnp.int32, (tm, 1), 0)
    if sample:
        tok = _token_rows(tm)
        xr = jnp.where(tok, xr, cache_ref[...])
        halo = jnp.zeros(halo_ref.shape, F32)
    else:
        halo = jnp.where(i == 0, 0.0, halo_ref[...])
    xc = _causal_conv(xr, halo, win_ref, cw_ref, cb_ref)
    r = jax.nn.sigmoid(_dot(xc, wa_ref[...]) + ba_ref[...])
    gi = jax.nn.sigmoid(_dot(xc, wx_ref[...]) + bx_ref[...])
    log_a = -RG_C * r * _softplus(-lam_ref[...])
    a = jnp.exp(log_a)
    bx = jnp.sqrt(_one_minus_exp(2.0 * log_a)) * (gi * xc)
    if sample:
        pos = row % SAMPLE_ROWS
        a = jnp.where(tok, a, 1.0)
        bx = jnp.where(tok, bx, jnp.where(pos == SAMPLE_PAD - 1, h0_ref[...], 0.0))
        _, hid = _linear_scan_rows(a, bx, SAMPLE_ROWS, pos)
        h_ref[...] = hid
        y_ref[...] = jnp.where(tok, hid * _gelu_tanh(gate_ref[...]), 0.0).astype(y_ref.dtype)
    else:
        @pl.when(i == 0)
        def _():
            carry_ref[...] = jnp.zeros(carry_ref.shape, F32)
        a_cum, b_cum = _linear_scan_rows(a, bx, tm, row)
        hid = a_cum * carry_ref[...] + b_cum
        last = hid[tm - 1:tm, :]
        carry_ref[...] = last
        y_ref[...] = (hid * _gelu_tanh(gate_ref[...])).astype(y_ref.dtype)

        @pl.when(i == pl.num_programs(2) - 1)
        def _():
            h_ref[0] = last


def _rg_mix(proj, cache_rows, h0_rows, cw, cb, wa, ba, wx, bx, lam, *, sample):
    dm = _dims()
    width, blk, nh = dm["rg_width"], dm["rg_block"], RG_HEADS
    nts = dm["nts"]
    nt = proj.shape[0]
    if sample:
        tm = _tile(nts, PROMPT_CHUNK, SAMPLE_ROWS)
        nb, nc = 1, nts // tm
    else:
        tm = _tile(SEQ, PROMPT_CHUNK, SUBLANES)
        nb, nc = BATCH, SEQ // tm
    hb = tm // HALO

    def rows(b, i):
        return b * nc + i

    in_specs = [pl.BlockSpec((tm, blk), lambda b, j, i: (rows(b, i), j)),
                pl.BlockSpec((tm, blk), lambda b, j, i: (rows(b, i), nh + j)),
                pl.BlockSpec((HALO, blk), lambda b, j, i: (jnp.maximum(rows(b, i) * hb - 1, 0), nh + j))]
    args = [proj, proj, proj]
    if sample:
        in_specs += [pl.BlockSpec((tm, blk), lambda b, j, i: (i, j))] * 2
        args += [cache_rows, h0_rows]
    vec = pl.BlockSpec((1, blk), lambda b, j, i: (0, j))
    mat = pl.BlockSpec((None, blk, blk), lambda b, j, i: (j, 0, 0))
    in_specs += [pl.BlockSpec((CONV_W, blk), lambda b, j, i: (0, j)), vec, mat, vec, mat, vec, vec]
    args += [cw, cb, wa, ba, wx, bx, lam]
    if sample:
        h_spec = pl.BlockSpec((tm, blk), lambda b, j, i: (i, j))
        h_shape = jax.ShapeDtypeStruct((nts, width), F32)
    else:
        h_spec = pl.BlockSpec((1, 1, blk), lambda b, j, i: (b, 0, j))
        h_shape = jax.ShapeDtypeStruct((BATCH, 1, width), F32)
    return pl.pallas_call(
        functools.partial(_rg_kernel, sample=sample),
        grid=(nb, nh, nc),
        in_specs=in_specs,
        out_specs=[pl.BlockSpec((tm, blk), lambda b, j, i: (rows(b, i), j)), h_spec],
        out_shape=[jax.ShapeDtypeStruct((nt, width), _mixer_out_dtype(tm)), h_shape],
        scratch_shapes=[pltpu.VMEM((HALO + tm, blk), F32), pltpu.VMEM((1, blk), F32)],
        compiler_params=_cparams(("parallel", "parallel", "arbitrary")),
        name="rglru_mix",
    )(*args)


def _rglru_mixer(hn, x, st, w_in, conv_w, conv_b, w_a, b_a, w_x, b_x, lam, w_out):
    dm = _dims()
    width = dm["rg_width"]
    h0, conv0 = st
    proj = _mm(hn, w_in[None], 0)
    proj_s = _pad_sample(proj)
    conv_p, conv_s = _conv_tails(proj, proj_s, width, width)
    row = lambda a: a.reshape(1, width)
    cache_rows = _sample_rows(conv0, SAMPLE_PAD - (CONV_W - 1))
    h0_rows = _sample_rows(h0[:, None, :], SAMPLE_PAD - 1)
    consts = (conv_w, row(conv_b), w_a, row(b_a), w_x, row(b_x), row(lam))
    y, h_p = _rg_mix(proj, None, None, *consts, sample=False)
    y_s, h_all = _rg_mix(proj_s, cache_rows, h0_rows, *consts, sample=True)
    h_s = h_all.reshape(DEC_BATCH, SAMPLE_ROWS, width)[:, SAMPLE_ROWS - 1]
    x = _mm(_merge_sample(y, y_s), w_out[None], 0, res=x, scale=1.0, tm_pref=256)
    return x, (h_p[:, 0], conv_p), (h_s, conv_s)


def _ssd_scan_kernel(*refs, chunk, pad, has_init, n_heads, rep, headdim):
    refs = list(refs)
    raw = [refs.pop(0) for _ in range(3)]
    halos = [refs.pop(0) for _ in range(3)]
    caches = [refs.pop(0) for _ in range(3)] if has_init else [None] * 3
    cws = [refs.pop(0) for _ in range(3)]
    cbs = [refs.pop(0) for _ in range(3)]
    z_ref, dt_ref, dtb_ref, alog_ref, dskip_ref, nw_ref = [refs.pop(0) for _ in range(6)]
    s0_ref = refs.pop(0) if has_init else None
    y_ref, s_out_ref, s_ref = refs[:3]
    wins = refs[3:6]
    acts = refs[6:9]
    xs_ref, bm_ref, cm_ref = acts
    g = pl.program_id(1)
    c = pl.program_id(2)
    gw = rep * headdim
    nseq = s_out_ref.shape[0]

    nrows = raw[0].shape[0]
    for x_ref, halo_ref, cache_ref, cw_ref, cb_ref, win_ref, act_ref in zip(raw, halos, caches, cws, cbs, wins, acts):
        x = x_ref[...]
        if has_init:
            x = jnp.where(_token_rows(nrows), x, cache_ref[...])
            halo = jnp.zeros(halo_ref.shape, F32)
        else:
            halo = jnp.where(c == 0, 0.0, halo_ref[...])
        act_ref[...] = _silu(_causal_conv(x, halo, win_ref, cw_ref, cb_ref))

    t_col = lax.broadcasted_iota(jnp.int32, (chunk, 1), 0)
    valid_col = t_col >= pad
    lane = lax.broadcasted_iota(jnp.int32, (chunk, LANES), 1)
    head_ok = valid_col & (lane < n_heads)
    ci = lax.broadcasted_iota(jnp.int32, (LANES, gw), 0)
    ji = lax.broadcasted_iota(jnp.int32, (LANES, gw), 1)
    spread = (ci == g * rep + ji // headdim).astype(F32)
    ci2 = lax.broadcasted_iota(jnp.int32, (LANES, LANES), 0)
    ri2 = lax.broadcasted_iota(jnp.int32, (LANES, LANES), 1)
    to_lanes = ((ci2 == g * rep + ri2) & (ri2 < rep)).astype(F32)
    r8 = lax.broadcasted_iota(jnp.int32, (SUBLANES, LANES), 0)
    c8 = lax.broadcasted_iota(jnp.int32, (SUBLANES, LANES), 1)
    to_sublanes = (c8 == g * rep + r8).astype(F32)
    t_mat = lax.broadcasted_iota(jnp.int32, (chunk, chunk), 0)
    s_mat = lax.broadcasted_iota(jnp.int32, (chunk, chunk), 1)
    tril = t_mat >= s_mat
    lane_head = lax.broadcasted_iota(jnp.int32, (1, gw), 1) // headdim
    neg_a = -jnp.exp(alog_ref[...])

    def one_seq(i, s):
        rows = slice(i * chunk, (i + 1) * chunk)
        dt = jnp.where(head_ok, _softplus(dt_ref[rows, :] + dtb_ref[...]), 0.0)
        log_a = jnp.where(head_ok, dt * neg_a, 0.0)
        bcum = _cumsum_rows(log_a)
        dt_w = _dot_exact(dt, spread)
        b_w = _dot_exact(bcum, spread)
        b_cols = _dot_exact(bcum, to_lanes)
        b_rows = _dot_nt_exact(to_sublanes, bcum)

        xs = xs_ref[rows, :]
        bm = bm_ref[rows, :]
        cm = cm_ref[rows, :]
        vals = xs * dt_w
        qk = _dot_nt(cm, bm)
        y = jnp.zeros((chunk, gw), F32)
        for r in range(rep):
            d = jnp.where(tril, b_cols[:, r:r + 1] - b_rows[r:r + 1, :], -jnp.inf)
            y = y + _dot(qk * jnp.exp(d), jnp.where(lane_head == r, vals, 0.0))
        b_last = b_w[chunk - 1:chunk, :]
        y = y + jnp.exp(b_w) * _dot_nt(cm, s)
        upd = _dot_tn(vals * jnp.exp(b_last - b_w), bm)
        s_new = jnp.concatenate(
            [jnp.exp(b_rows[r:r + 1, chunk - 1:chunk]) * s[r * headdim:(r + 1) * headdim, :]
             + upd[r * headdim:(r + 1) * headdim, :] for r in range(rep)], axis=0)

        y = (y + dskip_ref[...] * xs) * _silu(z_ref[rows, :])
        yn = y * lax.rsqrt(jnp.mean(y * y, axis=-1, keepdims=True) + EPS) * nw_ref[...]
        y_ref[rows, :] = jnp.where(valid_col, yn, 0.0).astype(y_ref.dtype)
        return s_new

    state_shape = s_out_ref.shape[1:]
    if has_init:
        for i in range(nseq):
            s_out_ref[i] = one_seq(i, s0_ref[i].reshape(s_ref.shape)).reshape(state_shape)
    else:
        @pl.when(c == 0)
        def _():
            s_ref[...] = jnp.zeros(s_ref.shape, F32)

        s_new = one_seq(0, s_ref[...])
        s_ref[...] = s_new

        @pl.when(c == pl.num_programs(2) - 1)
        def _():
            s_out_ref[0] = s_new.reshape(state_shape)


def _ssd_scan(proj, cache_rows, cw, cb, dtb, alog, dskip, norm_w, s0, *, nb, nc, chunk, pad, nseq=1):
    dm = _dims()
    inner, nheads, rep = dm["ssd_inner"], dm["ssd_heads"], dm["ssd_rep"]
    assert rep == SUBLANES and nheads <= LANES and SSD_STATE == LANES
    assert nseq == 1 or (nc == 1 and s0 is not None)
    gw = rep * SSD_HEADDIM
    ng = SSD_GROUPS
    nt = proj.shape[0]
    dt_blk = (inner + dm["ssd_conv"]) // LANES
    rb = nseq * chunk
    hb = rb // HALO
    parts = [(gw, lambda g: g), (SSD_STATE, lambda g: inner // SSD_STATE + g),
             (SSD_STATE, lambda g: inner // SSD_STATE + ng + g)]

    def loc(b, c):
        return b * nc + c

    def spec(rows, w, row_fn, col_fn):
        return pl.BlockSpec((rows, w), lambda b, g, c: (row_fn(b, c), col_fn(g)))

    in_specs, args = [], []
    for w, cblk in parts:
        in_specs.append(spec(rb, w, loc, lambda g, w=w, cblk=cblk: inner // w + cblk(g)))
        args.append(proj)
    for w, cblk in parts:
        in_specs.append(spec(HALO, w, lambda b, c: jnp.maximum(loc(b, c) * hb - 1, 0),
                             lambda g, w=w, cblk=cblk: inner // w + cblk(g)))
        args.append(proj)
    if s0 is not None:
        for w, cblk in parts:
            in_specs.append(spec(rb, w, loc, cblk))
            args.append(cache_rows)
    for arr, rows in ((cw, CONV_W), (cb, 1)):
        for w, cblk in parts:
            in_specs.append(spec(rows, w, lambda b, c: 0, cblk))
            args.append(arr)
    in_specs += [pl.BlockSpec((rb, gw), lambda b, g, c: (loc(b, c), g)),
                 pl.BlockSpec((rb, LANES), lambda b, g, c: (loc(b, c), dt_blk)),
                 pl.BlockSpec((1, LANES), lambda b, g, c: (0, 0)),
                 pl.BlockSpec((1, LANES), lambda b, g, c: (0, 0)),
                 pl.BlockSpec((1, gw), lambda b, g, c: (0, g)),
                 pl.BlockSpec((1, gw), lambda b, g, c: (0, g))]
    args += [proj, proj, dtb, alog, dskip, norm_w]
    state_spec = pl.BlockSpec((nseq, rep, SSD_HEADDIM, SSD_STATE), lambda b, g, c: (b, g, 0, 0))
    state_shape = jax.ShapeDtypeStruct((nb * nseq, nheads, SSD_HEADDIM, SSD_STATE), F32)
    if s0 is not None:
        in_specs.append(state_spec)
        args.append(s0)
    return pl.pallas_call(
        functools.partial(_ssd_scan_kernel, chunk=chunk, pad=pad, has_init=s0 is not None,
                          n_heads=nheads, rep=rep, headdim=SSD_HEADDIM),
        grid=(nb, ng, nc),
        in_specs=in_specs,
        out_specs=[pl.BlockSpec((rb, gw), lambda b, g, c: (loc(b, c), g)), state_spec],
        out_shape=[jax.ShapeDtypeStruct((nt, inner), _mixer_out_dtype(chunk)), state_shape],
        scratch_shapes=([pltpu.VMEM((gw, SSD_STATE), F32)]
                        + [pltpu.VMEM((HALO + rb, w), F32) for w, _ in parts]
                        + [pltpu.VMEM((rb, w), F32) for w, _ in parts]),
        compiler_params=_cparams(("parallel", "parallel", "arbitrary")),
        name="ssd_scan",
    )(*args)


def _ssd_mixer(hn, x, st, w_in, conv_w, conv_b, dt_bias, a_log, d_skip, norm_w, w_out):
    dm = _dims()
    inner, cdim, nheads = dm["ssd_inner"], dm["ssd_conv"], dm["ssd_heads"]
    s0, conv0 = st
    proj = _mm(hn, w_in[None], 0)
    proj_s = _pad_sample(proj)
    conv_p, conv_s = _conv_tails(proj, proj_s, inner, cdim)
    cb = conv_b.reshape(1, cdim)
    cache_rows = _sample_rows(conv0, SAMPLE_PAD - (CONV_W - 1))
    padl = lambda a: jnp.pad(a, (0, LANES - nheads)).reshape(1, LANES)
    dtb, alog = padl(dt_bias), padl(a_log)
    dskip = jnp.repeat(d_skip, SSD_HEADDIM).reshape(1, inner)
    nw = norm_w.reshape(1, inner)

    chunk = _tile(SEQ, PROMPT_CHUNK, SUBLANES)
    y, s_p = _ssd_scan(proj, None, conv_w, cb, dtb, alog, dskip, nw, None,
                       nb=BATCH, nc=SEQ // chunk, chunk=chunk, pad=0)
    nseq = _tile(DEC_BATCH, SAMPLE_SEQS_PER_STEP, 1)
    y_s, s_s = _ssd_scan(proj_s, cache_rows, conv_w, cb, dtb, alog, dskip, nw, jnp.swapaxes(s0, 2, 3),
                         nb=DEC_BATCH // nseq, nc=1, chunk=SAMPLE_ROWS, pad=SAMPLE_PAD, nseq=nseq)
    x = _mm(_merge_sample(y, y_s), w_out[None], 0, res=x, scale=1.0, tm_pref=256)
    return x, (jnp.swapaxes(s_p, 2, 3), conv_p), (jnp.swapaxes(s_s, 2, 3), conv_s)


def kernel(x_prompt, x_sample, state_ret, state_mlstm_c, state_mlstm_n, state_mlstm_m, cache_mlstm_conv,
           state_rglru_h, cache_rglru_conv, state_ssd, cache_ssd_conv,
           norm_ffn1, w_ffn1_up, w_ffn1_down, norm_mix, norm_ffn2, w_ffn2_up, w_ffn2_down, norm_final,
           ret_w_in, ret_norm, ret_w_out,
           ml_w_up, ml_conv_w, ml_conv_b, ml_w_q, ml_w_k, ml_w_v, ml_w_gates, ml_b_gates, ml_skip, ml_norm, ml_w_down,
           rg_w_in, rg_conv_w, rg_conv_b, rg_w_a, rg_b_a, rg_w_x, rg_b_x, rg_lambda, rg_w_out,
           ssd_w_in, ssd_conv_w, ssd_conv_b, ssd_dt_bias, ssd_a_log, ssd_d, ssd_norm, ssd_w_out):
    d = D_MODEL
    ntp = BATCH * SEQ
    x = jnp.concatenate([x_prompt.reshape(ntp, d), x_sample.reshape(DEC_BATCH * DEC_SEQ, d)], axis=0)

    n1 = norm_ffn1.reshape(DEPTH, 1, d)
    nm = norm_mix.reshape(DEPTH, 1, d)
    n2 = norm_ffn2.reshape(DEPTH, 1, d)
    states_p = {}
    states_s = {}
    for i in range(DEPTH):
        x = _ffn(x, n1, w_ffn1_up, w_ffn1_down, i)
        hn = _rmsnorm(x, nm, i, MXU_DTYPE)
        kind = i % N_MIXERS
        if kind == 0:
            x, states_p["ret"], states_s["ret"] = _retention_mixer(hn, x, state_ret, ret_w_in, ret_norm, ret_w_out)
        elif kind == 1:
            x, states_p["ml"], states_s["ml"] = _mlstm_mixer(
                hn, x, (state_mlstm_c, state_mlstm_n, state_mlstm_m, cache_mlstm_conv),
                ml_w_up, ml_conv_w, ml_conv_b, ml_w_q, ml_w_k, ml_w_v, ml_w_gates, ml_b_gates, ml_skip, ml_norm,
                ml_w_down)
        elif kind == 2:
            x, states_p["rg"], states_s["rg"] = _rglru_mixer(
                hn, x, (state_rglru_h, cache_rglru_conv), rg_w_in, rg_conv_w, rg_conv_b, rg_w_a, rg_b_a,
                rg_w_x, rg_b_x, rg_lambda, rg_w_out)
        else:
            x, states_p["ssd"], states_s["ssd"] = _ssd_mixer(
                hn, x, (state_ssd, cache_ssd_conv), ssd_w_in, ssd_conv_w, ssd_conv_b, ssd_dt_bias, ssd_a_log,
                ssd_d, ssd_norm, ssd_w_out)
        x = _ffn(x, n2, w_ffn2_up, w_ffn2_down, i)
    y = _rmsnorm(x, norm_final.reshape(1, 1, d), 0, F32)
    y_prompt = y[:ntp].reshape(BATCH, SEQ, d)
    y_sample = y[ntp:].reshape(DEC_BATCH, DEC_SEQ, d)

    def flat(st):
        return (st["ret"],) + tuple(st["ml"]) + tuple(st["rg"]) + tuple(st["ssd"])

    return (y_prompt, y_sample) + flat(states_p) + flat(states_s)
```

```python
import functools
import math

import jax
import jax.numpy as jnp
from jax import lax
from jax.experimental import pallas as pl
from jax.experimental.pallas import tpu as pltpu

D_MODEL = 2048
BATCH = 4
SEQ = 2048
DEPTH = 4
DEC_BATCH = 128
DEC_SEQ = 4
PAST_LEN = 16384
N_MIXERS = 4
D_FF = 5632
EPS = 1e-6
CONV_W = 4

RET_HEADS = 8
ROPE_BASE = 10000.0
ML_HEADS = 4
ML_QKV_BLOCK = 4
RG_HEADS = 10
RG_C = 8.0
SSD_HEADDIM = 64
SSD_GROUPS = 8
SSD_STATE = 128

F32 = jnp.float32
MXU_DTYPE = jnp.bfloat16

SUBLANES = 8
LANES = 128
MXU_DIM = 256
VMEM_LIMIT = 56 * 1024 * 1024

SAMPLE_ROWS = SUBLANES
SAMPLE_PAD = SAMPLE_ROWS - DEC_SEQ
HALO = SUBLANES

ROW_TILE = 1088
NORM_ROW_TILE = 512
COL_TILE = 512
WIDE_COL_TILE = 1024
SAMPLE_SEQS_PER_STEP = 8
PROMPT_CHUNK = 256
PRE_ROW_TILE = 128


def _dims():
    d = D_MODEL
    ret_dk = d // RET_HEADS
    ml_inner = 2 * d
    rg_width = d * 5 // 4
    ssd_inner = 2 * d
    ssd_heads = ssd_inner // SSD_HEADDIM
    return dict(
        ret_dk=ret_dk, ret_dv=2 * ret_dk,
        ml_inner=ml_inner, ml_dh=ml_inner // ML_HEADS,
        rg_width=rg_width, rg_block=rg_width // RG_HEADS,
        ssd_inner=ssd_inner, ssd_heads=ssd_heads, ssd_rep=ssd_heads // SSD_GROUPS,
        ssd_conv=ssd_inner + 2 * SSD_GROUPS * SSD_STATE,
        ntp=BATCH * SEQ, nts=DEC_BATCH * SAMPLE_ROWS,
    )


def _tile(n, pref, mult):
    t = min(pref, n) // mult * mult
    while t > mult and n % t:
        t -= mult
    assert t >= mult and n % t == 0, (n, pref, mult)
    return t


def _cparams(sem):
    return pltpu.CompilerParams(dimension_semantics=sem, vmem_limit_bytes=VMEM_LIMIT)


def _silu(x):
    return x * jax.nn.sigmoid(x)


def _softplus(x):
    return jnp.maximum(x, 0.0) + jnp.log1p(jnp.exp(-jnp.abs(x)))


def _dot(a, b):
    return jnp.dot(a.astype(MXU_DTYPE), b.astype(MXU_DTYPE), preferred_element_type=F32)


def _dot_nt(a, b):
    return lax.dot_general(a.astype(MXU_DTYPE), b.astype(MXU_DTYPE), (((1,), (1,)), ((), ())),
                           preferred_element_type=F32)


def _dot_tn(a, b):
    return lax.dot_general(a.astype(MXU_DTYPE), b.astype(MXU_DTYPE), (((0,), (0,)), ((), ())),
                           preferred_element_type=F32)


def _split3(x):
    hi = x.astype(jnp.bfloat16)
    r1 = x - hi.astype(F32)
    mid = r1.astype(jnp.bfloat16)
    lo = (r1 - mid.astype(F32)).astype(jnp.bfloat16)
    return hi, mid, lo


def _dot_exact(a, sel):
    sel = sel.astype(jnp.bfloat16)
    hi, mid, lo = (jnp.dot(p, sel, preferred_element_type=F32) for p in _split3(a))
    return (hi + mid) + lo


def _dot_nt_exact(sel, b):
    sel = sel.astype(jnp.bfloat16)
    dims = (((1,), (1,)), ((), ()))
    hi, mid, lo = (lax.dot_general(sel, p, dims, preferred_element_type=F32) for p in _split3(b))
    return (hi + mid) + lo


def _cumsum_rows(x):
    n = x.shape[0]
    row = lax.broadcasted_iota(jnp.int32, x.shape, 0)
    sh = 1
    while sh < n:
        x = x + jnp.where(row >= sh, pltpu.roll(x, sh, 0), 0.0)
        sh *= 2
    return x


def _linear_scan_rows(a, b, span, pos):
    sh = 1
    while sh < span:
        ok = pos >= sh
        a_prev = jnp.where(ok, pltpu.roll(a, sh, 0), 1.0)
        b_prev = jnp.where(ok, pltpu.roll(b, sh, 0), 0.0)
        b = b + a * b_prev
        a = a * a_prev
        sh *= 2
    return a, b


def _causal_conv(x, halo, win_ref, w_ref, b_ref):
    tm = x.shape[0]
    win_ref[0:HALO, :] = halo
    win_ref[HALO:HALO + tm, :] = x
    y = b_ref[...] + x * w_ref[CONV_W - 1:CONV_W, :]
    for j in range(1, CONV_W):
        y = y + win_ref[pl.ds(HALO - j, tm), :] * w_ref[CONV_W - 1 - j:CONV_W - j, :]
    return y


def _mixer_out_dtype(block_rows):
    packed_rows = SUBLANES * (4 // jnp.dtype(MXU_DTYPE).itemsize)
    return MXU_DTYPE if block_rows % packed_rows == 0 else F32


def _token_rows(n):
    row = lax.broadcasted_iota(jnp.int32, (n, 1), 0)
    return (row % SAMPLE_ROWS) >= SAMPLE_PAD


def _rmsnorm_kernel(x_ref, w_ref, o_ref):
    x = x_ref[...]
    ms = jnp.mean(x * x, axis=-1, keepdims=True)
    o_ref[...] = (x * lax.rsqrt(ms + EPS) * w_ref[...]).astype(o_ref.dtype)


def _rmsnorm(x, w, layer, out_dtype, row0=0, nrows=None):
    d = x.shape[1]
    nrows = x.shape[0] if nrows is None else nrows
    tm = _tile(math.gcd(row0, nrows), NORM_ROW_TILE, 16)
    blk0 = row0 // tm
    return pl.pallas_call(
        _rmsnorm_kernel,
        grid=(nrows // tm,),
        in_specs=[pl.BlockSpec((tm, d), lambda i: (blk0 + i, 0)),
                  pl.BlockSpec((None, 1, d), lambda i: (layer, 0, 0))],
        out_specs=pl.BlockSpec((tm, d), lambda i: (i, 0)),
        out_shape=jax.ShapeDtypeStruct((nrows, d), out_dtype),
        compiler_params=_cparams(("parallel",)),
        name="rmsnorm",
    )(x, w)


def _mm_kernel(a_ref, w_ref, *rest, scale, has_res):
    if has_res:
        r_ref, o_ref, wb_ref = rest
    else:
        o_ref, wb_ref = rest

    @pl.when(pl.program_id(1) == 0)
    def _():
        wb_ref[...] = w_ref[...].astype(MXU_DTYPE)

    acc = jnp.dot(a_ref[...].astype(MXU_DTYPE), wb_ref[...], preferred_element_type=F32)
    if has_res:
        acc = r_ref[...] + scale * acc
    o_ref[...] = acc.astype(o_ref.dtype)


def _mm(a, w, layer, *, res=None, scale=1.0, tm_pref=ROW_TILE, tn_pref=WIDE_COL_TILE):
    m, k = a.shape
    n = w.shape[2]
    tm = _tile(m, tm_pref, 16)
    tn = min(tn_pref, n)
    nj = pl.cdiv(n, tn)
    w_mode = dict(pipeline_mode=pl.Buffered(1)) if res is not None else {}
    in_specs = [pl.BlockSpec((tm, k), lambda j, i: (i, 0)),
                pl.BlockSpec((None, k, tn), lambda j, i: (layer, 0, j), **w_mode)]
    args = [a, w]
    aliases = {}
    if res is not None:
        assert n % tn == 0 and res.shape == (m, n)
        in_specs.append(pl.BlockSpec((tm, tn), lambda j, i: (i, j)))
        args.append(res)
        aliases = {2: 0}
    return pl.pallas_call(
        functools.partial(_mm_kernel, scale=scale, has_res=res is not None),
        grid=(nj, m // tm),
        in_specs=in_specs,
        out_specs=pl.BlockSpec((tm, tn), lambda j, i: (i, j)),
        out_shape=jax.ShapeDtypeStruct((m, nj * tn), F32),
        scratch_shapes=[pltpu.VMEM((k, tn), MXU_DTYPE)],
        input_output_aliases=aliases,
        compiler_params=_cparams(("parallel", "arbitrary")),
        name="matmul",
    )(*args)


def _swiglu_up_kernel(a_ref, wg_ref, wu_ref, o_ref, wgb_ref, wub_ref):
    @pl.when(pl.program_id(1) == 0)
    def _():
        wgb_ref[...] = wg_ref[...].astype(MXU_DTYPE)
        wub_ref[...] = wu_ref[...].astype(MXU_DTYPE)

    a = a_ref[...]
    gate = jnp.dot(a, wgb_ref[...], preferred_element_type=F32)
    up = jnp.dot(a, wub_ref[...], preferred_element_type=F32)
    o_ref[...] = (_silu(gate) * up).astype(o_ref.dtype)


def _swiglu_up(a, w_up, layer):
    m, k = a.shape
    ff = w_up.shape[2] // 2
    tm = _tile(m, ROW_TILE, 16)
    tn = _tile(ff, COL_TILE, LANES)
    nj = ff // tn
    return pl.pallas_call(
        _swiglu_up_kernel,
        grid=(nj, m // tm),
        in_specs=[pl.BlockSpec((tm, k), lambda j, i: (i, 0)),
                  pl.BlockSpec((None, k, tn), lambda j, i: (layer, 0, j)),
                  pl.BlockSpec((None, k, tn), lambda j, i: (layer, 0, nj + j))],
        out_specs=pl.BlockSpec((tm, tn), lambda j, i: (i, j)),
        out_shape=jax.ShapeDtypeStruct((m, ff), MXU_DTYPE),
        scratch_shapes=[pltpu.VMEM((k, tn), MXU_DTYPE), pltpu.VMEM((k, tn), MXU_DTYPE)],
        compiler_params=_cparams(("parallel", "arbitrary")),
        name="swiglu_up",
    )(a, w_up, w_up)


def _ffn(x, norm_w, w_up, w_down, layer):
    h = _rmsnorm(x, norm_w, layer, MXU_DTYPE)
    act = _swiglu_up(h, w_up, layer)
    return _mm(act, w_down, layer, res=x, scale=0.5, tm_pref=256)


def _ret_scan_kernel(lg_ref, q_ref, k_ref, v_ref, g_ref, cos_ref, sin_ref, nw_ref, *rest,
                     chunk, pad, has_init, k_scale):
    rest = list(rest)
    s0_ref = rest.pop(0) if has_init else None
    y_ref, s_out_ref, s_ref = rest
    h = pl.program_id(1)
    c = pl.program_id(2)
    nvalid = chunk - pad
    nseq = s_out_ref.shape[0]
    lam = lg_ref[h]
    half = cos_ref.shape[1]

    t_col = lax.broadcasted_iota(jnp.int32, (chunk, 1), 0)
    cnt_col = jnp.maximum(t_col - pad + 1, 0).astype(F32)
    t_mat = lax.broadcasted_iota(jnp.int32, (chunk, chunk), 0)
    s_mat = lax.broadcasted_iota(jnp.int32, (chunk, chunk), 1)
    cnt_t = jnp.maximum(t_mat - pad + 1, 0).astype(F32)
    cnt_s = jnp.maximum(s_mat - pad + 1, 0).astype(F32)
    w = jnp.where((t_mat >= s_mat) & (s_mat >= pad), jnp.exp(lam * (cnt_t - cnt_s)), 0.0)
    ws = jnp.where(t_col >= pad, jnp.exp(lam * (nvalid - cnt_col)), 0.0)
    w_in = jnp.exp(lam * cnt_col)
    w_state = jnp.exp(lam * nvalid)

    def one_seq(i, s):
        r = slice(i * chunk, (i + 1) * chunk)
        cos = cos_ref[r, :]
        sin = sin_ref[r, :]

        def rot(x):
            x1 = x[:, :half]
            x2 = x[:, half:]
            return jnp.concatenate([x1 * cos - x2 * sin, x1 * sin + x2 * cos], axis=-1)

        q = rot(q_ref[r, :])
        k = rot(k_ref[r, :]) * k_scale
        v = v_ref[r, :]
        y = _dot(_dot_nt(q, k) * w, v) + w_in * _dot(q, s)
        s_new = w_state * s + _dot_tn(k * ws, v)
        mu = jnp.mean(y, axis=-1, keepdims=True)
        yc = y - mu
        yn = yc * lax.rsqrt(jnp.mean(yc * yc, axis=-1, keepdims=True) + EPS) * nw_ref[...]
        out = _silu(g_ref[r, :]) * yn
        y_ref[r, :] = jnp.where(t_col >= pad, out, 0.0).astype(y_ref.dtype)
        return s_new

    if has_init:
        for i in range(nseq):
            s_out_ref[i, 0] = one_seq(i, s0_ref[i, 0])
    else:
        @pl.when(c == 0)
        def _():
            s_ref[...] = jnp.zeros(s_ref.shape, F32)

        s_new = one_seq(0, s_ref[...])
        s_ref[...] = s_new

        @pl.when(c == pl.num_programs(2) - 1)
        def _():
            s_out_ref[0, 0] = s_new


def _ret_scan(proj, cos, sin, log_gamma, norm_w, s0, *, nb, nc, chunk, pad, nseq=1):
    dm = _dims()
    dk, dv, nh = dm["ret_dk"], dm["ret_dv"], RET_HEADS
    nt = proj.shape[0]
    assert nseq == 1 or (nc == 1 and s0 is not None)
    rb = nseq * chunk
    y_dtype = _mixer_out_dtype(chunk)

    def rows(b, c):
        return b * nc + c

    in_specs = [pl.BlockSpec(memory_space=pltpu.SMEM),
                pl.BlockSpec((rb, dk), lambda b, h, c: (rows(b, c), h)),
                pl.BlockSpec((rb, dk), lambda b, h, c: (rows(b, c), nh + h)),
                pl.BlockSpec((rb, dv), lambda b, h, c: (rows(b, c), nh + h)),
                pl.BlockSpec((rb, dv), lambda b, h, c: (rows(b, c), 2 * nh + h)),
                pl.BlockSpec((rb, dk // 2), lambda b, h, c: (rows(b, c), 0)),
                pl.BlockSpec((rb, dk // 2), lambda b, h, c: (rows(b, c), 0)),
                pl.BlockSpec((1, dv), lambda b, h, c: (0, h))]
    args = [log_gamma, proj, proj, proj, proj, cos, sin, norm_w]
    if s0 is not None:
        in_specs.append(pl.BlockSpec((nseq, 1, dk, dv), lambda b, h, c: (b, h, 0, 0)))
        args.append(s0)
    return pl.pallas_call(
        functools.partial(_ret_scan_kernel, chunk=chunk, pad=pad, has_init=s0 is not None, k_scale=dk ** -0.5),
        grid=(nb, nh, nc),
        in_specs=in_specs,
        out_specs=[pl.BlockSpec((rb, dv), lambda b, h, c: (rows(b, c), h)),
                   pl.BlockSpec((nseq, 1, dk, dv), lambda b, h, c: (b, h, 0, 0))],
        out_shape=[jax.ShapeDtypeStruct((nt, nh * dv), y_dtype),
                   jax.ShapeDtypeStruct((nb * nseq, nh, dk, dv), F32)],
        scratch_shapes=[pltpu.VMEM((dk, dv), F32)],
        compiler_params=_cparams(("parallel", "parallel", "arbitrary")),
        name="retention_scan",
    )(*args)


def _pad_sample(a):
    ntp = BATCH * SEQ
    s = lax.slice(a, (ntp, 0), (ntp + DEC_BATCH * DEC_SEQ, a.shape[1]))
    s = jnp.pad(s.reshape(DEC_BATCH, DEC_SEQ, a.shape[1]), ((0, 0), (SAMPLE_PAD, 0), (0, 0)))
    return s.reshape(DEC_BATCH * SAMPLE_ROWS, a.shape[1])


def _merge_sample(y, y_s):
    ntp = BATCH * SEQ
    tok = y_s.reshape(DEC_BATCH, SAMPLE_ROWS, y_s.shape[1])[:, SAMPLE_PAD:].astype(y.dtype)
    return lax.dynamic_update_slice(y, tok.reshape(DEC_BATCH * DEC_SEQ, y_s.shape[1]), (ntp, 0))


def _retention_mixer(hn, x, state_ret, w_in, norm_w, w_out):
    dm = _dims()
    dk = dm["ret_dk"]
    proj = _mm(hn, w_in[None], 0)
    proj_s = _pad_sample(proj)

    half = dk // 2
    inv = ROPE_BASE ** (-jnp.arange(half, dtype=F32) / half)

    def tables(pos):
        ang = pos.astype(F32)[:, None] * inv
        return jnp.cos(ang), jnp.sin(ang)

    cos_p, sin_p = tables(jnp.tile(jnp.arange(SEQ), BATCH))
    cos_s, sin_s = tables(jnp.tile(PAST_LEN - SAMPLE_PAD + jnp.arange(SAMPLE_ROWS), DEC_BATCH))
    log_gamma = jnp.log1p(-(2.0 ** (-5.0 - jnp.arange(RET_HEADS, dtype=F32))))
    nw = norm_w.reshape(1, -1)

    chunk = _tile(SEQ, PROMPT_CHUNK, SUBLANES)
    y, s_p = _ret_scan(proj, cos_p, sin_p, log_gamma, nw, None,
                       nb=BATCH, nc=SEQ // chunk, chunk=chunk, pad=0)
    nseq = _tile(DEC_BATCH, SAMPLE_SEQS_PER_STEP, 1)
    y_s, s_s = _ret_scan(proj_s, cos_s, sin_s, log_gamma, nw, state_ret,
                         nb=DEC_BATCH // nseq, nc=1, chunk=SAMPLE_ROWS, pad=SAMPLE_PAD, nseq=nseq)
    x = _mm(_merge_sample(y, y_s), w_out[None], 0, res=x, scale=1.0, tm_pref=256)
    return x, s_p, s_s


def _ml_pre_kernel(xm_ref, halo_ref, *rest, sample, seq_len, k_scale, nblk, bw):
    rest = list(rest)
    cache_ref = rest.pop(0) if sample else None
    (cw_ref, cb_ref, wq_ref, wk_ref, wv_ref, wg_ref, bg_ref,
     q_ref, k_ref, v_ref, xc_ref, gates_ref, win_ref) = rest
    tm = xm_ref.shape[0]
    xm = xm_ref[...]
    if sample:
        xm = jnp.where(_token_rows(tm), xm, cache_ref[...])
        halo = jnp.zeros(halo_ref.shape, F32)
    else:
        seq_start = (pl.program_id(0) * tm) % seq_len == 0
        halo = jnp.where(seq_start, 0.0, halo_ref[...])
    xc = _silu(_causal_conv(xm, halo, win_ref, cw_ref, cb_ref))
    xc_ref[...] = xc

    for j in range(nblk):
        sl = slice(j * bw, (j + 1) * bw)
        q_ref[:, sl] = _dot(xc[:, sl], wq_ref[j]).astype(q_ref.dtype)
        k_ref[:, sl] = _dot(xc[:, sl], wk_ref[j]).astype(k_ref.dtype)
        v_ref[:, sl] = _dot(xm[:, sl], wv_ref[j]).astype(v_ref.dtype)
    inner = q_ref.shape[1]
    g = (_dot(q_ref[...], wg_ref[0:inner, :]) + _dot(k_ref[...], wg_ref[inner:2 * inner, :])
         + _dot(v_ref[...], wg_ref[2 * inner:3 * inner, :]) + bg_ref[...])
    k_ref[...] = (k_ref[...] * k_scale).astype(k_ref.dtype)
    lane = lax.broadcasted_iota(jnp.int32, g.shape, 1)
    log_f = jnp.minimum(g, 0.0) - jnp.log1p(jnp.exp(-jnp.abs(g)))
    gates_ref[...] = jnp.where(lane < ML_HEADS, g, jnp.where(lane < 2 * ML_HEADS, log_f, 0.0))


def _ml_pre(up, cache_rows, cw, cb, wq, wk, wv, wg, bg, *, sample, chunk):
    dm = _dims()
    inner, dh = dm["ml_inner"], dm["ml_dh"]
    n = dm["nts"] if sample else dm["ntp"]
    tm = _tile(n if sample else SEQ, PRE_ROW_TILE, SAMPLE_ROWS)
    hb = tm // HALO
    nblk, bw = wq.shape[0], wq.shape[1]

    in_specs = [pl.BlockSpec((tm, inner), lambda i: (i, 0)),
                pl.BlockSpec((HALO, inner), lambda i: (jnp.maximum(i * hb - 1, 0), 0))]
    args = [up, up]
    if sample:
        in_specs.append(pl.BlockSpec((tm, inner), lambda i: (i, 0)))
        args.append(cache_rows)
    const2 = lambda i: (0, 0)
    const3 = lambda i: (0, 0, 0)
    in_specs += [pl.BlockSpec(cw.shape, const2), pl.BlockSpec(cb.shape, const2),
                 pl.BlockSpec(wq.shape, const3), pl.BlockSpec(wk.shape, const3), pl.BlockSpec(wv.shape, const3),
                 pl.BlockSpec(wg.shape, const2), pl.BlockSpec(bg.shape, const2)]
    args += [cw, cb, wq, wk, wv, wg, bg]
    row_spec = pl.BlockSpec((tm, inner), lambda i: (i, 0))
    return pl.pallas_call(
        functools.partial(_ml_pre_kernel, sample=sample, seq_len=SEQ, k_scale=dh ** -0.5, nblk=nblk, bw=bw),
        grid=(n // tm,),
        in_specs=in_specs,
        out_specs=[row_spec, row_spec, row_spec, row_spec, pl.BlockSpec((tm, LANES), lambda i: (i, 0))],
        out_shape=([jax.ShapeDtypeStruct((n, inner), _mixer_out_dtype(chunk))] * 3
                   + [jax.ShapeDtypeStruct((n, inner), F32), jax.ShapeDtypeStruct((n, LANES), F32)]),
        scratch_shapes=[pltpu.VMEM((HALO + tm, inner), F32)],
        compiler_params=_cparams(("arbitrary",)),
        name="mlstm_pre",
    )(*args)


def _ml_scan_kernel(m0_ref, q_ref, k_ref, v_ref, xc_ref, z_ref, gates_ref, skip_ref, nw_ref, *rest,
                    chunk, pad, has_init):
    rest = list(rest)
    if has_init:
        c0_ref, n0_ref = rest.pop(0), rest.pop(0)
    y_ref, c_out_ref, n_out_ref, m_out_ref, c_ref, n_ref, m_ref = rest
    b = pl.program_id(0)
    h = pl.program_id(1)
    c = pl.program_id(2)

    @pl.when(c == 0)
    def _():
        if has_init:
            c_ref[...] = c0_ref[0, 0]
            n_ref[...] = n0_ref[0, 0]
            m_ref[...] = jnp.full(m_ref.shape, m0_ref[b * ML_HEADS + h], F32)
        else:
            c_ref[...] = jnp.zeros(c_ref.shape, F32)
            n_ref[...] = jnp.zeros(n_ref.shape, F32)
            m_ref[...] = jnp.zeros(m_ref.shape, F32)

    q = q_ref[...]
    k = k_ref[...]
    v = v_ref[...]
    t_col = lax.broadcasted_iota(jnp.int32, (chunk, 1), 0)
    valid_col = t_col >= pad

    gates = jnp.where(valid_col, gates_ref[...], 0.0)
    csum = _cumsum_rows(gates)
    lane = lax.broadcasted_iota(jnp.int32, gates.shape, 1)
    ig_col = jnp.sum(jnp.where(lane == h, gates, 0.0), axis=1, keepdims=True)
    b_col = jnp.sum(jnp.where(lane == ML_HEADS + h, csum, 0.0), axis=1, keepdims=True)
    ig_col = jnp.where(valid_col, ig_col, -jnp.inf)
    sel = (lax.broadcasted_iota(jnp.int32, (SUBLANES, LANES), 0)
           == lax.broadcasted_iota(jnp.int32, (SUBLANES, LANES), 1)).astype(F32)
    sub = lax.broadcasted_iota(jnp.int32, (SUBLANES, chunk), 0)
    ig_row = jnp.sum(jnp.where(sub == h, _dot_nt_exact(sel, gates), 0.0), axis=0, keepdims=True)
    b_row = jnp.sum(jnp.where(sub == ML_HEADS + h, _dot_nt_exact(sel, csum), 0.0), axis=0, keepdims=True)
    s_row = lax.broadcasted_iota(jnp.int32, (1, chunk), 1)
    ig_row = jnp.where(s_row >= pad, ig_row, -jnp.inf)

    t_mat = lax.broadcasted_iota(jnp.int32, (chunk, chunk), 0)
    s_mat = lax.broadcasted_iota(jnp.int32, (chunk, chunk), 1)
    d = jnp.where(t_mat >= s_mat, b_col - b_row + ig_row, -jnp.inf)
    m_prev = m_ref[...][:, 0:1]
    inter = b_col + m_prev
    m_t = jnp.maximum(inter, jnp.max(d, axis=1, keepdims=True))
    sc = _dot_nt(q, k) * jnp.exp(d - m_t)
    w_inter = jnp.exp(inter - m_t)
    cmat = c_ref[...]
    nvec = n_ref[...]
    num = _dot(sc, v) + w_inter * _dot(q, cmat)
    den = jnp.sum(sc, axis=1, keepdims=True) + w_inter * jnp.sum(q * nvec, axis=1, keepdims=True)
    hid = num / jnp.maximum(jnp.abs(den), jnp.exp(-m_t))

    m_new = m_t[chunk - 1:chunk, :]
    b_last = b_col[chunk - 1:chunk, :]
    ws = jnp.exp(b_last - b_col + ig_col - m_new)
    decay = jnp.exp(b_last + m_prev - m_new)
    kw = k * ws
    c_new = decay * cmat + _dot_tn(kw, v)
    n_new = decay * nvec + jnp.sum(kw, axis=0, keepdims=True)
    c_ref[...] = c_new
    n_ref[...] = n_new
    m_ref[...] = jnp.broadcast_to(m_new, m_ref.shape)

    mu = jnp.mean(hid, axis=-1, keepdims=True)
    hc = hid - mu
    hn = hc * lax.rsqrt(jnp.mean(hc * hc, axis=-1, keepdims=True) + EPS) * nw_ref[...]
    out = (hn + skip_ref[...] * xc_ref[...]) * _silu(z_ref[...])
    y_ref[...] = jnp.where(valid_col, out, 0.0).astype(y_ref.dtype)

    @pl.when(c == pl.num_programs(2) - 1)
    def _():
        c_out_ref[0, 0] = c_new
        n_out_ref[0, 0] = n_new
        m_out_ref[0, 0] = jnp.broadcast_to(m_new, (1, LANES))


def _ml_scan(q, k, v, xc, up, gates, skip, norm_w, init, *, nb, nc, chunk, pad):
    dm = _dims()
    dh, nh, inner = dm["ml_dh"], ML_HEADS, dm["ml_inner"]
    nt = up.shape[0]

    def loc(b, h, c):
        return (b * nc + c, h)

    head_spec = pl.BlockSpec((chunk, dh), loc)
    in_specs = [pl.BlockSpec(memory_space=pltpu.SMEM),
                head_spec, head_spec, head_spec, head_spec,
                pl.BlockSpec((chunk, dh), lambda b, h, c: (b * nc + c, nh + h)),
                pl.BlockSpec((chunk, LANES), lambda b, h, c: (b * nc + c, 0)),
                pl.BlockSpec((1, dh), lambda b, h, c: (0, h)),
                pl.BlockSpec((1, dh), lambda b, h, c: (0, h))]
    has_init = init is not None
    if has_init:
        c0, n0, m0 = init
    else:
        c0 = n0 = None
        m0 = jnp.zeros((1,), F32)
    args = [m0, q, k, v, xc, up, gates, skip, norm_w]
    if has_init:
        in_specs += [pl.BlockSpec((1, 1, dh, dh), lambda b, h, c: (b, h, 0, 0)),
                     pl.BlockSpec((1, 1, 1, dh), lambda b, h, c: (b, h, 0, 0))]
        args += [c0, n0]
    return pl.pallas_call(
        functools.partial(_ml_scan_kernel, chunk=chunk, pad=pad, has_init=has_init),
        grid=(nb, nh, nc),
        in_specs=in_specs,
        out_specs=[pl.BlockSpec((chunk, dh), loc),
                   pl.BlockSpec((1, 1, dh, dh), lambda b, h, c: (b, h, 0, 0)),
                   pl.BlockSpec((1, 1, 1, dh), lambda b, h, c: (b, h, 0, 0)),
                   pl.BlockSpec((1, 1, 1, LANES), lambda b, h, c: (b, h, 0, 0))],
        out_shape=[jax.ShapeDtypeStruct((nt, inner), _mixer_out_dtype(chunk)),
                   jax.ShapeDtypeStruct((nb, nh, dh, dh), F32),
                   jax.ShapeDtypeStruct((nb, nh, 1, dh), F32),
                   jax.ShapeDtypeStruct((nb, nh, 1, LANES), F32)],
        scratch_shapes=[pltpu.VMEM((dh, dh), F32), pltpu.VMEM((1, dh), F32), pltpu.VMEM((1, LANES), F32)],
        compiler_params=_cparams(("parallel", "parallel", "arbitrary")),
        name="mlstm_scan",
    )(*args)


def _expand_block_diag(w, bw):
    nblk, bs, _ = w.shape
    per = bw // bs
    wt = w.reshape(nblk // per, per, bs, bs)
    eye = jnp.eye(per, dtype=w.dtype)
    return jnp.einsum("npio,pq->npiqo", wt, eye).reshape(nblk // per, bw, bw).astype(MXU_DTYPE)


def _sample_rows(a, lead):
    r = a.shape[1]
    a = jnp.pad(a, ((0, 0), (lead, SAMPLE_ROWS - lead - r), (0, 0)))
    return a.reshape(DEC_BATCH * SAMPLE_ROWS, a.shape[2])


def _conv_tails(proj, proj_s, col0, width):
    tail = CONV_W - 1
    p = jnp.stack([lax.slice(proj, ((b + 1) * SEQ - tail, col0), ((b + 1) * SEQ, col0 + width))
                   for b in range(BATCH)])
    s = lax.slice(proj_s, (0, col0), (proj_s.shape[0], col0 + width))
    s = s.reshape(DEC_BATCH, SAMPLE_ROWS, width)[:, SAMPLE_ROWS - tail:]
    return p, s


def _mlstm_mixer(hn, x, st, w_up, conv_w, conv_b, w_q, w_k, w_v, w_gates, b_gates, skip, norm_w, w_down):
    dm = _dims()
    inner, dh = dm["ml_inner"], dm["ml_dh"]
    c0, n0, m0, conv0 = st
    up = _mm(hn, w_up[None], 0)
    up_s = _pad_sample(up)
    conv_p, conv_s = _conv_tails(up, up_s, 0, inner)

    bw = MXU_DIM if inner % MXU_DIM == 0 else LANES
    wq, wk, wv = (_expand_block_diag(w, bw) for w in (w_q, w_k, w_v))
    wg = jnp.pad(w_gates, ((0, 0), (0, LANES - w_gates.shape[1]))).astype(MXU_DTYPE)
    bg = jnp.pad(b_gates, (0, LANES - b_gates.shape[0])).reshape(1, LANES)
    cb = conv_b.reshape(1, inner)
    cache_rows = _sample_rows(conv0, SAMPLE_PAD - (CONV_W - 1))
    skip2 = skip.reshape(1, inner)
    nw = norm_w.reshape(1, inner)

    chunk = _tile(SEQ, PROMPT_CHUNK, SUBLANES)
    qp, kp, vp, xcp, gp = _ml_pre(up, None, conv_w, cb, wq, wk, wv, wg, bg, sample=False, chunk=chunk)
    y, c_p, n_p, m_p = _ml_scan(qp, kp, vp, xcp, up, gp, skip2, nw, None,
                                nb=BATCH, nc=SEQ // chunk, chunk=chunk, pad=0)
    qs, ks, vs, xcs, gs = _ml_pre(up_s, cache_rows, conv_w, cb, wq, wk, wv, wg, bg, sample=True,
                                  chunk=SAMPLE_ROWS)
    y_s, c_s, n_s, m_s = _ml_scan(qs, ks, vs, xcs, up_s, gs, skip2, nw,
                                  (c0, n0.reshape(DEC_BATCH, ML_HEADS, 1, dh), m0.reshape(-1)),
                                  nb=DEC_BATCH, nc=1, chunk=SAMPLE_ROWS, pad=SAMPLE_PAD)
    x = _mm(_merge_sample(y, y_s), w_down[None], 0, res=x, scale=1.0, tm_pref=256)
    prompt_state = (c_p, n_p[:, :, 0], m_p[:, :, 0, 0], conv_p)
    sample_state = (c_s, n_s[:, :, 0], m_s[:, :, 0, 0], conv_s)
    return x, prompt_state, sample_state


def _gelu_tanh(x):
    return 0.5 * x * (1.0 + jnp.tanh(0.7978845608028654 * (x + 0.044715 * x * x * x)))


def _one_minus_exp(y):
    series = -y * (1.0 + y * (0.5 + y * (1.0 / 6.0 + y * (1.0 / 24.0 + y * (1.0 / 120.0)))))
    return jnp.where(y > -0.01, series, 1.0 - jnp.exp(y))


def _rg_kernel(gate_ref, xr_ref, halo_ref, *rest, sample):
    rest = list(rest)
    if sample:
        cache_ref, h0_ref = rest.pop(0), rest.pop(0)
    (cw_ref, cb_ref, wa_ref, ba_ref, wx_ref, bx_ref, lam_ref, y_ref, h_ref, win_ref, carry_ref) = rest
    tm = xr_ref.shape[0]
    i = pl.program_id(2)
    xr = xr_ref[...]
    row = lax.broadcasted_iota(jnp.int32, (tm, 1), 0)
    if sample:
        tok = _token_rows(tm)
        xr = jnp.where(tok, xr, cache_ref[...])
        halo = jnp.zeros(halo_ref.shape, F32)
    else:
        halo = jnp.where(i == 0, 0.0, halo_ref[...])
    xc = _causal_conv(xr, halo, win_ref, cw_ref, cb_ref)
    r = jax.nn.sigmoid(_dot(xc, wa_ref[...]) + ba_ref[...])
    gi = jax.nn.sigmoid(_dot(xc, wx_ref[...]) + bx_ref[...])
    log_a = -RG_C * r * _softplus(-lam_ref[...])
    a = jnp.exp(log_a)
    bx = jnp.sqrt(_one_minus_exp(2.0 * log_a)) * (gi * xc)
    if sample:
        pos = row % SAMPLE_ROWS
        a = jnp.where(tok, a, 1.0)
        bx = jnp.where(tok, bx, jnp.where(pos == SAMPLE_PAD - 1, h0_ref[...], 0.0))
        _, hid = _linear_scan_rows(a, bx, SAMPLE_ROWS, pos)
        h_ref[...] = hid
        y_ref[...] = jnp.where(tok, hid * _gelu_tanh(gate_ref[...]), 0.0).astype(y_ref.dtype)
    else:
        @pl.when(i == 0)
        def _():
            carry_ref[...] = jnp.zeros(carry_ref.shape, F32)
        a_cum, b_cum = _linear_scan_rows(a, bx, tm, row)
        hid = a_cum * carry_ref[...] + b_cum
        last = hid[tm - 1:tm, :]
        carry_ref[...] = last
        y_ref[...] = (hid * _gelu_tanh(gate_ref[...])).astype(y_ref.dtype)

        @pl.when(i == pl.num_programs(2) - 1)
        def _():
            h_ref[0] = last


def _rg_mix(proj, cache_rows, h0_rows, cw, cb, wa, ba, wx, bx, lam, *, sample):
    dm = _dims()
    width, blk, nh = dm["rg_width"], dm["rg_block"], RG_HEADS
    nts = dm["nts"]
    nt = proj.shape[0]
    if sample:
        tm = _tile(nts, PROMPT_CHUNK, SAMPLE_ROWS)
        nb, nc = 1, nts // tm
    else:
        tm = _tile(SEQ, PROMPT_CHUNK, SUBLANES)
        nb, nc = BATCH, SEQ // tm
    hb = tm // HALO

    def rows(b, i):
        return b * nc + i

    in_specs = [pl.BlockSpec((tm, blk), lambda b, j, i: (rows(b, i), j)),
                pl.BlockSpec((tm, blk), lambda b, j, i: (rows(b, i), nh + j)),
                pl.BlockSpec((HALO, blk), lambda b, j, i: (jnp.maximum(rows(b, i) * hb - 1, 0), nh + j))]
    args = [proj, proj, proj]
    if sample:
        in_specs += [pl.BlockSpec((tm, blk), lambda b, j, i: (i, j))] * 2
        args += [cache_rows, h0_rows]
    vec = pl.BlockSpec((1, blk), lambda b, j, i: (0, j))
    mat = pl.BlockSpec((None, blk, blk), lambda b, j, i: (j, 0, 0))
    in_specs += [pl.BlockSpec((CONV_W, blk), lambda b, j, i: (0, j)), vec, mat, vec, mat, vec, vec]
    args += [cw, cb, wa, ba, wx, bx, lam]
    if sample:
        h_spec = pl.BlockSpec((tm, blk), lambda b, j, i: (i, j))
        h_shape = jax.ShapeDtypeStruct((nts, width), F32)
    else:
        h_spec = pl.BlockSpec((1, 1, blk), lambda b, j, i: (b, 0, j))
        h_shape = jax.ShapeDtypeStruct((BATCH, 1, width), F32)
    return pl.pallas_call(
        functools.partial(_rg_kernel, sample=sample),
        grid=(nb, nh, nc),
        in_specs=in_specs,
        out_specs=[pl.BlockSpec((tm, blk), lambda b, j, i: (rows(b, i), j)), h_spec],
        out_shape=[jax.ShapeDtypeStruct((nt, width), _mixer_out_dtype(tm)), h_shape],
        scratch_shapes=[pltpu.VMEM((HALO + tm, blk), F32), pltpu.VMEM((1, blk), F32)],
        compiler_params=_cparams(("parallel", "parallel", "arbitrary")),
        name="rglru_mix",
    )(*args)


def _rglru_mixer(hn, x, st, w_in, conv_w, conv_b, w_a, b_a, w_x, b_x, lam, w_out):
    dm = _dims()
    width = dm["rg_width"]
    h0, conv0 = st
    proj = _mm(hn, w_in[None], 0)
    proj_s = _pad_sample(proj)
    conv_p, conv_s = _conv_tails(proj, proj_s, width, width)
    row = lambda a: a.reshape(1, width)
    cache_rows = _sample_rows(conv0, SAMPLE_PAD - (CONV_W - 1))
    h0_rows = _sample_rows(h0[:, None, :], SAMPLE_PAD - 1)
    consts = (conv_w, row(conv_b), w_a, row(b_a), w_x, row(b_x), row(lam))
    y, h_p = _rg_mix(proj, None, None, *consts, sample=False)
    y_s, h_all = _rg_mix(proj_s, cache_rows, h0_rows, *consts, sample=True)
    h_s = h_all.reshape(DEC_BATCH, SAMPLE_ROWS, width)[:, SAMPLE_ROWS - 1]
    x = _mm(_merge_sample(y, y_s), w_out[None], 0, res=x, scale=1.0, tm_pref=256)
    return x, (h_p[:, 0], conv_p), (h_s, conv_s)


def _ssd_scan_kernel(*refs, chunk, pad, has_init, n_heads, rep, headdim):
    refs = list(refs)
    raw = [refs.pop(0) for _ in range(3)]
    halos = [refs.pop(0) for _ in range(3)]
    caches = [refs.pop(0) for _ in range(3)] if has_init else [None] * 3
    cws = [refs.pop(0) for _ in range(3)]
    cbs = [refs.pop(0) for _ in range(3)]
    z_ref, dt_ref, dtb_ref, alog_ref, dskip_ref, nw_ref = [refs.pop(0) for _ in range(6)]
    s0_ref = refs.pop(0) if has_init else None
    y_ref, s_out_ref, s_ref = refs[:3]
    wins = refs[3:6]
    acts = refs[6:9]
    xs_ref, bm_ref, cm_ref = acts
    g = pl.program_id(1)
    c = pl.program_id(2)
    gw = rep * headdim
    nseq = s_out_ref.shape[0]

    nrows = raw[0].shape[0]
    for x_ref, halo_ref, cache_ref, cw_ref, cb_ref, win_ref, act_ref in zip(raw, halos, caches, cws, cbs, wins, acts):
        x = x_ref[...]
        if has_init:
            x = jnp.where(_token_rows(nrows), x, cache_ref[...])
            halo = jnp.zeros(halo_ref.shape, F32)
        else:
            halo = jnp.where(c == 0, 0.0, halo_ref[...])
        act_ref[...] = _silu(_causal_conv(x, halo, win_ref, cw_ref, cb_ref))

    t_col = lax.broadcasted_iota(jnp.int32, (chunk, 1), 0)
    valid_col = t_col >= pad
    lane = lax.broadcasted_iota(jnp.int32, (chunk, LANES), 1)
    head_ok = valid_col & (lane < n_heads)
    ci = lax.broadcasted_iota(jnp.int32, (LANES, gw), 0)
    ji = lax.broadcasted_iota(jnp.int32, (LANES, gw), 1)
    spread = (ci == g * rep + ji // headdim).astype(F32)
    ci2 = lax.broadcasted_iota(jnp.int32, (LANES, LANES), 0)
    ri2 = lax.broadcasted_iota(jnp.int32, (LANES, LANES), 1)
    to_lanes = ((ci2 == g * rep + ri2) & (ri2 < rep)).astype(F32)
    r8 = lax.broadcasted_iota(jnp.int32, (SUBLANES, LANES), 0)
    c8 = lax.broadcasted_iota(jnp.int32, (SUBLANES, LANES), 1)
    to_sublanes = (c8 == g * rep + r8).astype(F32)
    t_mat = lax.broadcasted_iota(jnp.int32, (chunk, chunk), 0)
    s_mat = lax.broadcasted_iota(jnp.int32, (chunk, chunk), 1)
    tril = t_mat >= s_mat
    lane_head = lax.broadcasted_iota(jnp.int32, (1, gw), 1) // headdim
    neg_a = -jnp.exp(alog_ref[...])

    def one_seq(i, s):
        rows = slice(i * chunk, (i + 1) * chunk)
        dt = jnp.where(head_ok, _softplus(dt_ref[rows, :] + dtb_ref[...]), 0.0)
        log_a = jnp.where(head_ok, dt * neg_a, 0.0)
        bcum = _cumsum_rows(log_a)
        dt_w = _dot_exact(dt, spread)
        b_w = _dot_exact(bcum, spread)
        b_cols = _dot_exact(bcum, to_lanes)
        b_rows = _dot_nt_exact(to_sublanes, bcum)

        xs = xs_ref[rows, :]
        bm = bm_ref[rows, :]
        cm = cm_ref[rows, :]
        vals = xs * dt_w
        qk = _dot_nt(cm, bm)
        y = jnp.zeros((chunk, gw), F32)
        for r in range(rep):
            d = jnp.where(tril, b_cols[:, r:r + 1] - b_rows[r:r + 1, :], -jnp.inf)
            y = y + _dot(qk * jnp.exp(d), jnp.where(lane_head == r, vals, 0.0))
        b_last = b_w[chunk - 1:chunk, :]
        y = y + jnp.exp(b_w) * _dot_nt(cm, s)
        upd = _dot_tn(vals * jnp.exp(b_last - b_w), bm)
        s_new = jnp.concatenate(
            [jnp.exp(b_rows[r:r + 1, chunk - 1:chunk]) * s[r * headdim:(r + 1) * headdim, :]
             + upd[r * headdim:(r + 1) * headdim, :] for r in range(rep)], axis=0)

        y = (y + dskip_ref[...] * xs) * _silu(z_ref[rows, :])
        yn = y * lax.rsqrt(jnp.mean(y * y, axis=-1, keepdims=True) + EPS) * nw_ref[...]
        y_ref[rows, :] = jnp.where(valid_col, yn, 0.0).astype(y_ref.dtype)
        return s_new

    state_shape = s_out_ref.shape[1:]
    if has_init:
        for i in range(nseq):
            s_out_ref[i] = one_seq(i, s0_ref[i].reshape(s_ref.shape)).reshape(state_shape)
    else:
        @pl.when(c == 0)
        def _():
            s_ref[...] = jnp.zeros(s_ref.shape, F32)

        s_new = one_seq(0, s_ref[...])
        s_ref[...] = s_new

        @pl.when(c == pl.num_programs(2) - 1)
        def _():
            s_out_ref[0] = s_new.reshape(state_shape)


def _ssd_scan(proj, cache_rows, cw, cb, dtb, alog, dskip, norm_w, s0, *, nb, nc, chunk, pad, nseq=1):
    dm = _dims()
    inner, nheads, rep = dm["ssd_inner"], dm["ssd_heads"], dm["ssd_rep"]
    assert rep == SUBLANES and nheads <= LANES and SSD_STATE == LANES
    assert nseq == 1 or (nc == 1 and s0 is not None)
    gw = rep * SSD_HEADDIM
    ng = SSD_GROUPS
    nt = proj.shape[0]
    dt_blk = (inner + dm["ssd_conv"]) // LANES
    rb = nseq * chunk
    hb = rb // HALO
    parts = [(gw, lambda g: g), (SSD_STATE, lambda g: inner // SSD_STATE + g),
             (SSD_STATE, lambda g: inner // SSD_STATE + ng + g)]

    def loc(b, c):
        return b * nc + c

    def spec(rows, w, row_fn, col_fn):
        return pl.BlockSpec((rows, w), lambda b, g, c: (row_fn(b, c), col_fn(g)))

    in_specs, args = [], []
    for w, cblk in parts:
        in_specs.append(spec(rb, w, loc, lambda g, w=w, cblk=cblk: inner // w + cblk(g)))
        args.append(proj)
    for w, cblk in parts:
        in_specs.append(spec(HALO, w, lambda b, c: jnp.maximum(loc(b, c) * hb - 1, 0),
                             lambda g, w=w, cblk=cblk: inner // w + cblk(g)))
        args.append(proj)
    if s0 is not None:
        for w, cblk in parts:
            in_specs.append(spec(rb, w, loc, cblk))
            args.append(cache_rows)
    for arr, rows in ((cw, CONV_W), (cb, 1)):
        for w, cblk in parts:
            in_specs.append(spec(rows, w, lambda b, c: 0, cblk))
            args.append(arr)
    in_specs += [pl.BlockSpec((rb, gw), lambda b, g, c: (loc(b, c), g)),
                 pl.BlockSpec((rb, LANES), lambda b, g, c: (loc(b, c), dt_blk)),
                 pl.BlockSpec((1, LANES), lambda b, g, c: (0, 0)),
                 pl.BlockSpec((1, LANES), lambda b, g, c: (0, 0)),
                 pl.BlockSpec((1, gw), lambda b, g, c: (0, g)),
                 pl.BlockSpec((1, gw), lambda b, g, c: (0, g))]
    args += [proj, proj, dtb, alog, dskip, norm_w]
    state_spec = pl.BlockSpec((nseq, rep, SSD_HEADDIM, SSD_STATE), lambda b, g, c: (b, g, 0, 0))
    state_shape = jax.ShapeDtypeStruct((nb * nseq, nheads, SSD_HEADDIM, SSD_STATE), F32)
    if s0 is not None:
        in_specs.append(state_spec)
        args.append(s0)
    return pl.pallas_call(
        functools.partial(_ssd_scan_kernel, chunk=chunk, pad=pad, has_init=s0 is not None,
                          n_heads=nheads, rep=rep, headdim=SSD_HEADDIM),
        grid=(nb, ng, nc),
        in_specs=in_specs,
        out_specs=[pl.BlockSpec((rb, gw), lambda b, g, c: (loc(b, c), g)), state_spec],
        out_shape=[jax.ShapeDtypeStruct((nt, inner), _mixer_out_dtype(chunk)), state_shape],
        scratch_shapes=([pltpu.VMEM((gw, SSD_STATE), F32)]
                        + [pltpu.VMEM((HALO + rb, w), F32) for w, _ in parts]
                        + [pltpu.VMEM((rb, w), F32) for w, _ in parts]),
        compiler_params=_cparams(("parallel", "parallel", "arbitrary")),
        name="ssd_scan",
    )(*args)


def _ssd_mixer(hn, x, st, w_in, conv_w, conv_b, dt_bias, a_log, d_skip, norm_w, w_out):
    dm = _dims()
    inner, cdim, nheads = dm["ssd_inner"], dm["ssd_conv"], dm["ssd_heads"]
    s0, conv0 = st
    proj = _mm(hn, w_in[None], 0)
    proj_s = _pad_sample(proj)
    conv_p, conv_s = _conv_tails(proj, proj_s, inner, cdim)
    cb = conv_b.reshape(1, cdim)
    cache_rows = _sample_rows(conv0, SAMPLE_PAD - (CONV_W - 1))
    padl = lambda a: jnp.pad(a, (0, LANES - nheads)).reshape(1, LANES)
    dtb, alog = padl(dt_bias), padl(a_log)
    dskip = jnp.repeat(d_skip, SSD_HEADDIM).reshape(1, inner)
    nw = norm_w.reshape(1, inner)

    chunk = _tile(SEQ, PROMPT_CHUNK, SUBLANES)
    y, s_p = _ssd_scan(proj, None, conv_w, cb, dtb, alog, dskip, nw, None,
                       nb=BATCH, nc=SEQ // chunk, chunk=chunk, pad=0)
    nseq = _tile(DEC_BATCH, SAMPLE_SEQS_PER_STEP, 1)
    y_s, s_s = _ssd_scan(proj_s, cache_rows, conv_w, cb, dtb, alog, dskip, nw, jnp.swapaxes(s0, 2, 3),
                         nb=DEC_BATCH // nseq, nc=1, chunk=SAMPLE_ROWS, pad=SAMPLE_PAD, nseq=nseq)
    x = _mm(_merge_sample(y, y_s), w_out[None], 0, res=x, scale=1.0, tm_pref=256)
    return x, (jnp.swapaxes(s_p, 2, 3), conv_p), (jnp.swapaxes(s_s, 2, 3), conv_s)


def kernel(x_prompt, x_sample, state_ret, state_mlstm_c, state_mlstm_n, state_mlstm_m, cache_mlstm_conv,
           state_rglru_h, cache_rglru_conv, state_ssd, cache_ssd_conv,
           norm_ffn1, w_ffn1_up, w_ffn1_down, norm_mix, norm_ffn2, w_ffn2_up, w_ffn2_down, norm_final,
           ret_w_in, ret_norm, ret_w_out,
           ml_w_up, ml_conv_w, ml_conv_b, ml_w_q, ml_w_k, ml_w_v, ml_w_gates, ml_b_gates, ml_skip, ml_norm, ml_w_down,
           rg_w_in, rg_conv_w, rg_conv_b, rg_w_a, rg_b_a, rg_w_x, rg_b_x, rg_lambda, rg_w_out,
           ssd_w_in, ssd_conv_w, ssd_conv_b, ssd_dt_bias, ssd_a_log, ssd_d, ssd_norm, ssd_w_out):
    d = D_MODEL
    ntp = BATCH * SEQ
    x = jnp.concatenate([x_prompt.reshape(ntp, d), x_sample.reshape(DEC_BATCH * DEC_SEQ, d)], axis=0)

    n1 = norm_ffn1.reshape(DEPTH, 1, d)
    nm = norm_mix.reshape(DEPTH, 1, d)
    n2 = norm_ffn2.reshape(DEPTH, 1, d)
    states_p = {}
    states_s = {}
    for i in range(DEPTH):
        x = _ffn(x, n1, w_ffn1_up, w_ffn1_down, i)
        hn = _rmsnorm(x, nm, i, MXU_DTYPE)
        kind = i % N_MIXERS
        if kind == 0:
            x, states_p["ret"], states_s["ret"] = _retention_mixer(hn, x, state_ret, ret_w_in, ret_norm, ret_w_out)
        elif kind == 1:
            x, states_p["ml"], states_s["ml"] = _mlstm_mixer(
                hn, x, (state_mlstm_c, state_mlstm_n, state_mlstm_m, cache_mlstm_conv),
                ml_w_up, ml_conv_w, ml_conv_b, ml_w_q, ml_w_k, ml_w_v, ml_w_gates, ml_b_gates, ml_skip, ml_norm,
                ml_w_down)
        elif kind == 2:
            x, states_p["rg"], states_s["rg"] = _rglru_mixer(
                hn, x, (state_rglru_h, cache_rglru_conv), rg_w_in, rg_conv_w, rg_conv_b, rg_w_a, rg_b_a,
                rg_w_x, rg_b_x, rg_lambda, rg_w_out)
        else:
            x, states_p["ssd"], states_s["ssd"] = _ssd_mixer(
                hn, x, (state_ssd, cache_ssd_conv), ssd_w_in, ssd_conv_w, ssd_conv_b, ssd_dt_bias, ssd_a_log,
                ssd_d, ssd_norm, ssd_w_out)
        x = _ffn(x, n2, w_ffn2_up, w_ffn2_down, i)
    nf = norm_final.reshape(1, 1, d)
    y_prompt = _rmsnorm(x, nf, 0, F32, 0, ntp).reshape(BATCH, SEQ, d)
    y_sample = _rmsnorm(x, nf, 0, F32, ntp, DEC_BATCH * DEC_SEQ).reshape(DEC_BATCH, DEC_SEQ, d)

    def flat(st):
        return (st["ret"],) + tuple(st["ml"]) + tuple(st["rg"]) + tuple(st["ssd"])

    return (y_prompt, y_sample) + flat(states_p) + flat(states_s)
```
